```python
import math
import jax, jax.numpy as jnp
from jax import lax
import numpy as np

D_MODEL = 2048
BATCH = 2
SEQ = 8192
DEPTH = 1
DEC_BATCH = 8
DEC_SEQ = 4096
PAST_LEN = 128

CONV_WIDTH = D_MODEL // 2
CONV_KERNEL = 31
RET_HEADS = 8
RET_QK_DIM = 128
RET_V_DIM = 256
RET_QK = RET_HEADS * RET_QK_DIM
RET_V = RET_HEADS * RET_V_DIM
CHUNK = 128
D_FF = -(-8 * D_MODEL // (3 * 256)) * 256
ROPE_THETA = 10000.0
EPS = 1e-6
GN_EPS = 1e-5
N_MOD = 6
IN_SIZES = (CONV_WIDTH, CONV_WIDTH, RET_QK, RET_QK, RET_V, RET_V, D_MODEL, D_MODEL)
IN_COLS = 2 * CONV_WIDTH + 2 * RET_QK + 2 * RET_V + 2 * D_MODEL

kernel_name = 'hybrid_conformer_retention_adaln_encoder'


def _rmsnorm(x, g):
    xf = x.astype(jnp.float32)
    y = xf * lax.rsqrt(jnp.mean(xf * xf, axis=-1, keepdims=True) + EPS) * g.astype(jnp.float32)
    return y.astype(x.dtype)


def _layernorm(x, g, b):
    xf = x.astype(jnp.float32)
    mu = jnp.mean(xf, axis=-1, keepdims=True)
    var = jnp.mean(jnp.square(xf - mu), axis=-1, keepdims=True)
    y = (xf - mu) * lax.rsqrt(var + GN_EPS) * g.astype(jnp.float32) + b.astype(jnp.float32)
    return y.astype(x.dtype)


def _rotary(x):
    L, dk = x.shape[1], x.shape[-1]
    half = dk // 2
    inv = ROPE_THETA ** (-jnp.arange(half, dtype=jnp.float32) / half)
    ang = jnp.arange(L, dtype=jnp.float32)[:, None] * inv[None, :]
    cos = jnp.cos(ang)[None, :, None, :]
    sin = jnp.sin(ang)[None, :, None, :]
    x1, x2 = x[..., :half], x[..., half:]
    return jnp.concatenate([x1 * cos - x2 * sin, x1 * sin + x2 * cos], axis=-1)


def _retention_dir(q, k, v, log_gamma):
    B, L, H, dk = q.shape
    dv = v.shape[-1]
    N = L // CHUNK
    qc = q.reshape(B, N, CHUNK, H, dk).transpose(0, 3, 1, 2, 4)
    kc = k.reshape(B, N, CHUNK, H, dk).transpose(0, 3, 1, 2, 4)
    vc = v.reshape(B, N, CHUNK, H, dv).transpose(0, 3, 1, 2, 4)
    idx = jnp.arange(CHUNK, dtype=jnp.float32)
    diff = idx[:, None] - idx[None, :]
    lower = diff >= 0
    dmask = jnp.where(lower[None], jnp.exp(jnp.where(lower, diff, 0.0)[None] * log_gamma[:, None, None]), 0.0)
    scores = jnp.einsum('bhnid,bhnjd->bhnij', qc, kc) * dmask[None, :, None]
    inner = jnp.einsum('bhnij,bhnje->bhnie', scores, vc)
    k_dec = kc * jnp.exp((CHUNK - 1 - idx)[None, :] * log_gamma[:, None])[None, :, None, :, None]
    kv = jnp.einsum('bhnjd,bhnje->bhnde', k_dec, vc)
    chunk_decay = jnp.exp(CHUNK * log_gamma)[None, :, None, None]

    def step(r, kv_n):
        return chunk_decay * r + kv_n, r

    _, r_prev = lax.scan(step, jnp.zeros((B, H, dk, dv), jnp.float32), jnp.moveaxis(kv, 2, 0))
    r_prev = jnp.moveaxis(r_prev, 0, 2)
    q_dec = qc * jnp.exp((idx + 1.0)[None, :] * log_gamma[:, None])[None, :, None, :, None]
    cross = jnp.einsum('bhnid,bhnde->bhnie', q_dec, r_prev)
    out = inner + cross
    return out.transpose(0, 2, 3, 1, 4).reshape(B, L, H, dv)


def _log_gamma(p):
    return jnp.log1p(-jnp.exp2(p.astype(jnp.float32)))


def _mixer(h, w_in, conv_dw_w, conv_dw_b, conv_ln_g, conv_ln_b, w_conv_out, b_conv_out,
           ret_logdecay_fwd, ret_logdecay_bwd, ret_gn_g, w_ret_out, w_out):
    B, L, _ = h.shape
    z = h @ w_in
    splits = []
    acc = 0
    for s in IN_SIZES[:-1]:
        acc += s
        splits.append(acc)
    a_val, a_gate, q, k, v, g_ret, gate_a, gate_b = jnp.split(z, splits, axis=-1)

    u = a_val * jax.nn.sigmoid(a_gate)
    pad = CONV_KERNEL // 2
    u = lax.conv_general_dilated(u, conv_dw_w[:, None, :], window_strides=(1,),
                                 padding=[(pad, pad)],
                                 dimension_numbers=('NWC', 'WIO', 'NWC'),
                                 feature_group_count=CONV_WIDTH) + conv_dw_b
    u = jax.nn.silu(_layernorm(u, conv_ln_g, conv_ln_b))
    y_a = u @ w_conv_out + b_conv_out

    qf = _rotary(q.astype(jnp.float32).reshape(B, L, RET_HEADS, RET_QK_DIM))
    kf = _rotary(k.astype(jnp.float32).reshape(B, L, RET_HEADS, RET_QK_DIM)) * (RET_QK_DIM ** -0.5)
    vf = v.astype(jnp.float32).reshape(B, L, RET_HEADS, RET_V_DIM)
    o = _retention_dir(qf, kf, vf, _log_gamma(ret_logdecay_fwd)) + \
        _retention_dir(qf[:, ::-1], kf[:, ::-1], vf[:, ::-1], _log_gamma(ret_logdecay_bwd))[:, ::-1]
    mu = jnp.mean(o, axis=-1, keepdims=True)
    var = jnp.mean(jnp.square(o - mu), axis=-1, keepdims=True)
    o = ((o - mu) * lax.rsqrt(var + GN_EPS)).reshape(B, L, RET_V) * ret_gn_g.astype(jnp.float32)
    o = o.astype(h.dtype) * jax.nn.silu(g_ret)
    y_b = o @ w_ret_out

    merged = jax.nn.sigmoid(gate_a) * y_a + jax.nn.sigmoid(gate_b) * y_b
    return merged @ w_out


def _ffn(h, w_ffn_in, w_ffn_out):
    gu = h @ w_ffn_in
    g, u = jnp.split(gu, 2, axis=-1)
    return (jax.nn.silu(g) * u) @ w_ffn_out


def _encoder(x, c, norm1_g, norm2_g, w_ada, b_ada, w_in, conv_dw_w, conv_dw_b, conv_ln_g,
             conv_ln_b, w_conv_out, b_conv_out, ret_logdecay_fwd, ret_logdecay_bwd, ret_gn_g,
             w_ret_out, w_out, w_ffn_in, w_ffn_out, final_g):
    for l in range(DEPTH):
        mod = jax.nn.silu(c) @ w_ada[l] + b_ada[l]
        sh1, sc1, gt1, sh2, sc2, gt2 = [m[:, None, :] for m in jnp.split(mod, N_MOD, axis=-1)]
        h = _rmsnorm(x, norm1_g[l]) * (1.0 + sc1) + sh1
        x = x + gt1 * _mixer(h, w_in[l], conv_dw_w[l], conv_dw_b[l], conv_ln_g[l], conv_ln_b[l],
                             w_conv_out[l], b_conv_out[l], ret_logdecay_fwd[l], ret_logdecay_bwd[l],
                             ret_gn_g[l], w_ret_out[l], w_out[l])
        h = _rmsnorm(x, norm2_g[l]) * (1.0 + sc2) + sh2
        x = x + gt2 * _ffn(h, w_ffn_in[l], w_ffn_out[l])
    return _rmsnorm(x, final_g)


def setup_inputs(seed: int = 0) -> dict:
    key = jax.random.key(seed)
    ks = jax.random.split(key, 24)
    f32 = jnp.float32
    nrm = lambda k, shape, s: jax.random.normal(k, shape, f32) * s
    head_base = -5.0 - jnp.arange(RET_HEADS, dtype=f32)
    return {
        'x_prompt': nrm(ks[0], (BATCH, SEQ, D_MODEL), 1.0),
        'x_sample': nrm(ks[1], (DEC_BATCH, DEC_SEQ, D_MODEL), 1.0),
        'c_prompt': nrm(ks[2], (BATCH, D_MODEL), 1.0),
        'c_sample': nrm(ks[3], (DEC_BATCH, D_MODEL), 1.0),
        'norm1_g': 1.0 + nrm(ks[4], (DEPTH, D_MODEL), 0.02),
        'norm2_g': 1.0 + nrm(ks[5], (DEPTH, D_MODEL), 0.02),
        'w_ada': nrm(ks[6], (DEPTH, D_MODEL, N_MOD * D_MODEL), 0.5 * D_MODEL ** -0.5),
        'b_ada': nrm(ks[7], (DEPTH, N_MOD * D_MODEL), 0.01),
        'w_in': nrm(ks[8], (DEPTH, D_MODEL, IN_COLS), D_MODEL ** -0.5),
        'conv_dw_w': nrm(ks[9], (DEPTH, CONV_KERNEL, CONV_WIDTH), CONV_KERNEL ** -0.5),
        'conv_dw_b': nrm(ks[10], (DEPTH, CONV_WIDTH), 0.01),
        'conv_ln_g': 1.0 + nrm(ks[11], (DEPTH, CONV_WIDTH), 0.02),
        'conv_ln_b': nrm(ks[12], (DEPTH, CONV_WIDTH), 0.01),
        'w_conv_out': nrm(ks[13], (DEPTH, CONV_WIDTH, D_MODEL), CONV_WIDTH ** -0.5),
        'b_conv_out': nrm(ks[14], (DEPTH, D_MODEL), 0.01),
        'ret_logdecay_fwd': head_base[None] + nrm(ks[15], (DEPTH, RET_HEADS), 0.1),
        'ret_logdecay_bwd': head_base[None] + nrm(ks[16], (DEPTH, RET_HEADS), 0.1),
        'ret_gn_g': 1.0 + nrm(ks[17], (DEPTH, RET_V), 0.02),
        'w_ret_out': nrm(ks[18], (DEPTH, RET_V, D_MODEL), RET_V ** -0.5),
        'w_out': nrm(ks[19], (DEPTH, D_MODEL, D_MODEL), D_MODEL ** -0.5),
        'w_ffn_in': nrm(ks[20], (DEPTH, D_MODEL, 2 * D_FF), D_MODEL ** -0.5),
        'w_ffn_out': nrm(ks[21], (DEPTH, D_FF, D_MODEL), D_FF ** -0.5),
        'final_g': 1.0 + nrm(ks[22], (D_MODEL,), 0.02),
    }


def reference(x_prompt, x_sample, c_prompt, c_sample, norm1_g, norm2_g, w_ada, b_ada, w_in,
              conv_dw_w, conv_dw_b, conv_ln_g, conv_ln_b, w_conv_out, b_conv_out,
              ret_logdecay_fwd, ret_logdecay_bwd, ret_gn_g, w_ret_out, w_out, w_ffn_in,
              w_ffn_out, final_g):
    y_prompt = _encoder(x_prompt, c_prompt, norm1_g, norm2_g, w_ada, b_ada, w_in, conv_dw_w,
                        conv_dw_b, conv_ln_g, conv_ln_b, w_conv_out, b_conv_out,
                        ret_logdecay_fwd, ret_logdecay_bwd, ret_gn_g, w_ret_out, w_out,
                        w_ffn_in, w_ffn_out, final_g)
    y_sample = _encoder(x_sample, c_sample, norm1_g, norm2_g, w_ada, b_ada, w_in, conv_dw_w,
                        conv_dw_b, conv_ln_g, conv_ln_b, w_conv_out, b_conv_out,
                        ret_logdecay_fwd, ret_logdecay_bwd, ret_gn_g, w_ret_out, w_out,
                        w_ffn_in, w_ffn_out, final_g)
    return (y_prompt, y_sample)
```

```python
import functools

import jax
import jax.numpy as jnp
from jax import lax
from jax.experimental import pallas as pl
from jax.experimental.pallas import tpu as pltpu

F32 = jnp.float32
BF16 = jnp.bfloat16

D_MODEL = 2048
CONV_WIDTH = D_MODEL // 2
CONV_KERNEL = 31
RET_HEADS = 8
RET_QK_DIM = 128
RET_V_DIM = 256
RET_QK = RET_HEADS * RET_QK_DIM
RET_V = RET_HEADS * RET_V_DIM
CHUNK = 128
D_FF = 5632
ROPE_THETA = 10000.0
EPS = 1e-6
GN_EPS = 1e-5
N_MOD = 6

Z_COLS = 11 * 1024
ZB_Q, ZB_K, ZB_U = 0, 1, 10
ZB2_V, ZB2_GR, ZB2_SGA, ZB2_SGB = 1, 2, 3, 4

MXU_N = 256
VMEM_LIMIT = 56 * 1024 * 1024
CONV_HALO = 16


def _params(sem):
    return pltpu.CompilerParams(dimension_semantics=sem, vmem_limit_bytes=VMEM_LIMIT)


def _sigmoid(x):
    return jax.nn.sigmoid(x)


def _silu(x):
    return x * jax.nn.sigmoid(x)


def _dot(a, b):
    return jnp.dot(a, b, preferred_element_type=F32)


def _mod_kernel(c_ref, w_ref, b_ref, o_ref):
    c = c_ref[...]
    s = _silu(c)
    o_ref[...] = _dot(s.astype(BF16), w_ref[...].astype(BF16)) + b_ref[...]


def _mod(c_all, w_ada, b_ada):
    rows = c_all.shape[0]
    tn = 1024
    return pl.pallas_call(
        _mod_kernel,
        grid=(N_MOD * D_MODEL // tn,),
        in_specs=[
            pl.BlockSpec((rows, D_MODEL), lambda j: (0, 0)),
            pl.BlockSpec((D_MODEL, tn), lambda j: (0, j)),
            pl.BlockSpec((1, tn), lambda j: (0, j)),
        ],
        out_specs=pl.BlockSpec((rows, tn), lambda j: (0, j)),
        out_shape=jax.ShapeDtypeStruct((rows, N_MOD * D_MODEL), F32),
        compiler_params=_params(("arbitrary",)),
        name="mod",
    )(c_all, w_ada, b_ada)


def _modnorm_rows(x, g, sc, sh):
    ms = jnp.mean(x * x, axis=-1, keepdims=True)
    y = x * lax.rsqrt(ms + EPS) * g
    return y * (1.0 + sc) + sh


def _modnorm_kernel(x_ref, g_ref, sc_ref, sh_ref, o_ref):
    o_ref[...] = _modnorm_rows(x_ref[...], g_ref[...], sc_ref[...], sh_ref[...]).astype(o_ref.dtype)


def _modnorm(x2d, g, mod3, seq, shift_blk, scale_blk):
    rows = x2d.shape[0]
    tm = 512
    per_seq = seq // tm
    return pl.pallas_call(
        _modnorm_kernel,
        grid=(rows // tm,),
        in_specs=[
            pl.BlockSpec((tm, D_MODEL), lambda i: (i, 0)),
            pl.BlockSpec((1, D_MODEL), lambda i: (0, 0)),
            pl.BlockSpec((None, 1, D_MODEL), lambda i: (i // per_seq, 0, scale_blk)),
            pl.BlockSpec((None, 1, D_MODEL), lambda i: (i // per_seq, 0, shift_blk)),
        ],
        out_specs=pl.BlockSpec((tm, D_MODEL), lambda i: (i, 0)),
        out_shape=jax.ShapeDtypeStruct((rows, D_MODEL), BF16),
        compiler_params=_params(("arbitrary",)),
        name="modnorm",
    )(x2d, g, mod3, mod3)


INPROJ_TM = 1024
INPROJ_TN = 1024


def _inproj_kernel(a_ref, w1_ref, w2_ref, cos_ref, sin_ref, o_ref):
    j = pl.program_id(1)
    nchunks = INPROJ_TN // MXU_N

    def z_chunk(w_ref, c):
        return _dot(a_ref[...], w_ref[:, c * MXU_N:(c + 1) * MXU_N])

    def store(c, val):
        o_ref[:, c * MXU_N:(c + 1) * MXU_N] = val.astype(o_ref.dtype)

    @pl.when(j <= ZB_K)
    def _rotary():
        scale = jnp.where(j == ZB_K, RET_QK_DIM ** -0.5, 1.0).astype(F32)
        cos2 = cos_ref[...]
        sin2 = sin_ref[...]
        for c in range(nchunks):
            z = z_chunk(w1_ref, c)
            outs = []
            for hh in range(MXU_N // RET_QK_DIM):
                zh = z[:, hh * RET_QK_DIM:(hh + 1) * RET_QK_DIM]
                rot = pltpu.roll(zh, RET_QK_DIM // 2, 1)
                outs.append((zh * cos2 + rot * sin2) * scale)
            store(c, jnp.concatenate(outs, axis=1))

    @pl.when((j > ZB_K) & (j < 4))
    def _identity():
        for c in range(nchunks):
            store(c, z_chunk(w1_ref, c))

    @pl.when((j >= 4) & (j < 6))
    def _swish():
        for c in range(nchunks):
            store(c, _silu(z_chunk(w1_ref, c)))

    @pl.when((j >= 6) & (j < ZB_U))
    def _gate():
        for c in range(nchunks):
            store(c, _sigmoid(z_chunk(w1_ref, c)))

    @pl.when(j == ZB_U)
    def _glu():
        for c in range(nchunks):
            store(c, z_chunk(w1_ref, c) * _sigmoid(z_chunk(w2_ref, c)))


def _inproj(h1, w_in_b, cos2, sin2, seq):
    rows = h1.shape[0]
    tm, tn = INPROJ_TM, INPROJ_TN
    per_seq = seq // tm
    nj = Z_COLS // tn
    w1_map = lambda i, j: (0, jnp.where(j == ZB_U, 0, j + 2))
    w2_map = lambda i, j: (0, 1)
    return pl.pallas_call(
        _inproj_kernel,
        grid=(rows // tm, nj),
        in_specs=[
            pl.BlockSpec((tm, D_MODEL), lambda i, j: (i, 0)),
            pl.BlockSpec((D_MODEL, tn), w1_map),
            pl.BlockSpec((D_MODEL, tn), w2_map),
            pl.BlockSpec((tm, RET_QK_DIM), lambda i, j: (i % per_seq, 0)),
            pl.BlockSpec((tm, RET_QK_DIM), lambda i, j: (i % per_seq, 0)),
        ],
        out_specs=pl.BlockSpec((tm, tn), lambda i, j: (i, j)),
        out_shape=jax.ShapeDtypeStruct((rows, Z_COLS), BF16),
        compiler_params=_params(("arbitrary", "arbitrary")),
        name="inproj",
    )(h1, w_in_b, w_in_b, cos2, sin2)


CONV_TM = 512
CONV_SUB = 8
CONV_LANES = 128
CONV_S = CONV_TM // CONV_SUB
CONV_PIN = CONV_S + 2 * CONV_HALO + 4
CONV_POUT = CONV_S + 4
CONV_RB = 16
CONV_SLABS = CONV_WIDTH // CONV_LANES
assert (CONV_PIN // 4) % 2 == 1 and (CONV_POUT // 4) % 2 == 1 and CONV_HALO >= CONV_KERNEL // 2


def _conv_kernel(up_ref, um_ref, un_ref, w_ref, b_ref, lg_ref, lb_ref, o_ref, pbuf, ybuf, obuf):
    i = pl.program_id(1)
    last = pl.num_programs(1) - 1
    S, H, PIN, POUT = CONV_S, CONV_HALO, CONV_PIN, CONV_POUT
    prev = jnp.where(i > 0, up_ref[...].astype(F32), 0.0)
    nxt = jnp.where(i < last, un_ref[...].astype(F32), 0.0)

    for s in range(CONV_SUB):
        for l in range(CONV_SLABS):
            lanes = slice(l * CONV_LANES, (l + 1) * CONV_LANES)
            left = prev[:, lanes] if s == 0 else um_ref[s * S - H:s * S, lanes].astype(F32)
            right = nxt[:, lanes] if s == CONV_SUB - 1 else um_ref[(s + 1) * S:(s + 1) * S + H, lanes].astype(F32)
            pbuf[l, s * PIN:s * PIN + H, :] = left
            pbuf[l, s * PIN + H:s * PIN + H + S, :] = um_ref[s * S:(s + 1) * S, lanes].astype(F32)
            pbuf[l, s * PIN + H + S:s * PIN + 2 * H + S, :] = right

    off = H - CONV_KERNEL // 2
    for l in range(CONV_SLABS):
        lanes = slice(l * CONV_LANES, (l + 1) * CONV_LANES)
        wv = [jnp.broadcast_to(w_ref[k:k + 1, lanes], (CONV_SUB, CONV_LANES)) for k in range(CONV_KERNEL)]
        bias = jnp.broadcast_to(b_ref[:, lanes], (CONV_SUB, CONV_LANES))

        def conv_body(jb, carry, l=l, wv=wv, bias=bias):
            j0 = jb * CONV_RB
            accs = [bias] * CONV_RB
            for mm in range(CONV_RB + CONV_KERNEL - 1):
                g = pbuf[l, pl.ds(j0 + off + mm, CONV_SUB, stride=PIN), :]
                for jj in range(CONV_RB):
                    k = mm - jj
                    if 0 <= k < CONV_KERNEL:
                        accs[jj] = accs[jj] + g * wv[k]
            for jj in range(CONV_RB):
                ybuf[l, pl.ds((j0 + jj) * CONV_SUB, CONV_SUB), :] = accs[jj]
            return carry

        lax.fori_loop(0, S // CONV_RB, conv_body, 0)

    lgs = [jnp.broadcast_to(lg_ref[:, l * CONV_LANES:(l + 1) * CONV_LANES], (CONV_SUB, CONV_LANES))
           for l in range(CONV_SLABS)]
    lbs = [jnp.broadcast_to(lb_ref[:, l * CONV_LANES:(l + 1) * CONV_LANES], (CONV_SUB, CONV_LANES))
           for l in range(CONV_SLABS)]

    def ln_body(j, carry):
        r0 = pl.multiple_of(j * CONV_SUB, CONV_SUB)
        ys = [ybuf[l, pl.ds(r0, CONV_SUB), :] for l in range(CONV_SLABS)]
        tot = ys[0]
        for y in ys[1:]:
            tot = tot + y
        mu = jnp.sum(tot, axis=-1, keepdims=True) * (1.0 / CONV_WIDTH)
        cs = [y - mu for y in ys]
        sq = cs[0] * cs[0]
        for c in cs[1:]:
            sq = sq + c * c
        var = jnp.sum(sq, axis=-1, keepdims=True) * (1.0 / CONV_WIDTH)
        inv = lax.rsqrt(var + GN_EPS)
        for l in range(CONV_SLABS):
            yn = cs[l] * inv * lgs[l] + lbs[l]
            obuf[l, pl.ds(j, CONV_SUB, stride=POUT), :] = _silu(yn)
        return carry

    lax.fori_loop(0, S, ln_body, 0, unroll=4)

    for s in range(CONV_SUB):
        for l in range(CONV_SLABS):
            o_ref[s * S:(s + 1) * S, l * CONV_LANES:(l + 1) * CONV_LANES] = (
                obuf[l, s * POUT:s * POUT + S, :].astype(o_ref.dtype))


def _conv(z, conv_w, conv_b, ln_g, ln_b, batch, seq):
    tm = CONV_TM
    per_seq = seq // tm
    hb = tm // CONV_HALO
    nhb = seq // CONV_HALO
    ub = ZB_U * (1024 // CONV_WIDTH)

    def prev_map(b, i):
        return (b * nhb + jnp.maximum(i * hb - 1, 0), ub)

    def next_map(b, i):
        return (b * nhb + jnp.minimum((i + 1) * hb, nhb - 1), ub)

    return pl.pallas_call(
        _conv_kernel,
        grid=(batch, per_seq),
        in_specs=[
            pl.BlockSpec((CONV_HALO, CONV_WIDTH), prev_map),
            pl.BlockSpec((tm, CONV_WIDTH), lambda b, i: (b * per_seq + i, ub)),
            pl.BlockSpec((CONV_HALO, CONV_WIDTH), next_map),
            pl.BlockSpec((CONV_KERNEL, CONV_WIDTH), lambda b, i: (0, 0)),
            pl.BlockSpec((1, CONV_WIDTH), lambda b, i: (0, 0)),
            pl.BlockSpec((1, CONV_WIDTH), lambda b, i: (0, 0)),
            pl.BlockSpec((1, CONV_WIDTH), lambda b, i: (0, 0)),
        ],
        out_specs=pl.BlockSpec((tm, CONV_WIDTH), lambda b, i: (b * per_seq + i, 0)),
        out_shape=jax.ShapeDtypeStruct((batch * seq, CONV_WIDTH), BF16),
        scratch_shapes=[
            pltpu.VMEM((CONV_SLABS, CONV_SUB * CONV_PIN, CONV_LANES), F32),
            pltpu.VMEM((CONV_SLABS, CONV_TM, CONV_LANES), F32),
            pltpu.VMEM((CONV_SLABS, CONV_SUB * CONV_POUT, CONV_LANES), F32),
        ],
        compiler_params=_params(("arbitrary", "arbitrary")),
        name="conv",
    )(z, z, z, conv_w, conv_b, ln_g, ln_b)


RS_CS = 4


def _log_gamma(p):
    return jnp.log1p(-jnp.exp2(p))


def _rstate_kernel(kf_ref, vf_ref, kb_ref, vb_ref, pf_ref, pb_ref, rf_ref, rb_ref, sf_ref, sb_ref):
    n = pl.program_id(1)

    @pl.when(n == 0)
    def _init():
        sf_ref[...] = jnp.zeros_like(sf_ref)
        sb_ref[...] = jnp.zeros_like(sb_ref)

    row = lax.broadcasted_iota(jnp.int32, (CHUNK, RET_QK_DIM), 0).astype(F32)
    tdims = (((0,), (0,)), ((), ()))

    def step(k_ref, v_ref, r_ref, s_ref, p_ref, c, fwd):
        rows = slice(c * CHUNK, (c + 1) * CHUNK)
        for h in range(RET_HEADS):
            lg = _log_gamma(p_ref[h])
            r_ref[c, h] = s_ref[h].astype(r_ref.dtype)
            kh = k_ref[rows, h * RET_QK_DIM:(h + 1) * RET_QK_DIM].astype(F32)
            pos = (CHUNK - 1.0 - row) if fwd else row
            kd = (kh * jnp.exp(pos * lg)).astype(BF16)
            vh = v_ref[rows, h * RET_V_DIM:(h + 1) * RET_V_DIM]
            kv = lax.dot_general(kd, vh, tdims, preferred_element_type=F32)
            s_ref[h] = s_ref[h] * jnp.exp(CHUNK * lg) + kv

    for c in range(RS_CS):
        step(kf_ref, vf_ref, rf_ref, sf_ref, pf_ref, c, True)
    for c in reversed(range(RS_CS)):
        step(kb_ref, vb_ref, rb_ref, sb_ref, pb_ref, c, False)


def _rstate(z, pf, pb, batch, seq):
    tr = RS_CS * CHUNK
    ns = seq // tr
    nchunks = batch * seq // CHUNK
    state = jax.ShapeDtypeStruct((nchunks, RET_HEADS, RET_QK_DIM, RET_V_DIM), BF16)
    fmap = lambda b, n: (b * ns + n)
    bmap = lambda b, n: (b * ns + ns - 1 - n)
    return pl.pallas_call(
        _rstate_kernel,
        grid=(batch, ns),
        in_specs=[
            pl.BlockSpec((tr, RET_QK), lambda b, n: (fmap(b, n), ZB_K)),
            pl.BlockSpec((tr, RET_V), lambda b, n: (fmap(b, n), ZB2_V)),
            pl.BlockSpec((tr, RET_QK), lambda b, n: (bmap(b, n), ZB_K)),
            pl.BlockSpec((tr, RET_V), lambda b, n: (bmap(b, n), ZB2_V)),
            pl.BlockSpec((RET_HEADS, 1, 1), lambda b, n: (0, 0, 0)),
            pl.BlockSpec((RET_HEADS, 1, 1), lambda b, n: (0, 0, 0)),
        ],
        out_specs=[
            pl.BlockSpec((RS_CS, RET_HEADS, RET_QK_DIM, RET_V_DIM), lambda b, n: (fmap(b, n), 0, 0, 0)),
            pl.BlockSpec((RS_CS, RET_HEADS, RET_QK_DIM, RET_V_DIM), lambda b, n: (bmap(b, n), 0, 0, 0)),
        ],
        out_shape=[state, state],
        scratch_shapes=[
            pltpu.VMEM((RET_HEADS, RET_QK_DIM, RET_V_DIM), F32),
            pltpu.VMEM((RET_HEADS, RET_QK_DIM, RET_V_DIM), F32),
        ],
        compiler_params=_params(("arbitrary", "arbitrary")),
        name="rstate",
    )(z, z, z, z, pf, pb)


RO_CS = 4


def _rout_kernel(q_ref, k_ref, v_ref, gr_ref, rf_ref, rb_ref, pf_ref, pb_ref, gn_ref, o_ref):
    ri = lax.broadcasted_iota(jnp.int32, (CHUNK, CHUNK), 0)
    ci = lax.broadcasted_iota(jnp.int32, (CHUNK, CHUNK), 1)
    diff = (ri - ci).astype(F32)
    row = ri.astype(F32)
    ndims = (((1,), (1,)), ((), ()))
    for h in range(RET_HEADS):
        lgf = _log_gamma(pf_ref[h])
        lgb = _log_gamma(pb_ref[h])
        lower = diff >= 0
        upper = diff <= 0
        mask = (jnp.where(lower, jnp.exp(jnp.where(lower, diff, 0.0) * lgf), 0.0)
                + jnp.where(upper, jnp.exp(jnp.where(upper, -diff, 0.0) * lgb), 0.0))
        qdec_f = jnp.exp((row + 1.0) * lgf)
        qdec_b = jnp.exp((CHUNK - row) * lgb)
        gn = gn_ref[:, h * RET_V_DIM:(h + 1) * RET_V_DIM]
        for c in range(RO_CS):
            rows = slice(c * CHUNK, (c + 1) * CHUNK)
            q = q_ref[rows, h * RET_QK_DIM:(h + 1) * RET_QK_DIM]
            k = k_ref[rows, h * RET_QK_DIM:(h + 1) * RET_QK_DIM]
            v = v_ref[rows, h * RET_V_DIM:(h + 1) * RET_V_DIM]
            s = lax.dot_general(q, k, ndims, preferred_element_type=F32)
            inner = _dot((s * mask).astype(BF16), v)
            qf = q.astype(F32)
            qd = jnp.concatenate([(qf * qdec_f).astype(BF16), (qf * qdec_b).astype(BF16)], axis=1)
            rcat = jnp.concatenate([rf_ref[c, h], rb_ref[c, h]], axis=0)
            o = inner + _dot(qd, rcat)
            mu = jnp.mean(o, axis=-1, keepdims=True)
            cen = o - mu
            var = jnp.mean(cen * cen, axis=-1, keepdims=True)
            on = cen * lax.rsqrt(var + GN_EPS) * gn
            gr = gr_ref[rows, h * RET_V_DIM:(h + 1) * RET_V_DIM].astype(F32)
            o_ref[rows, h * RET_V_DIM:(h + 1) * RET_V_DIM] = (on * gr).astype(o_ref.dtype)


def _rout(z, rf, rb, pf, pb, gn_g):
    rows = z.shape[0]
    tr = RO_CS * CHUNK
    state_spec = pl.BlockSpec((RO_CS, RET_HEADS, RET_QK_DIM, RET_V_DIM), lambda i: (i, 0, 0, 0))
    return pl.pallas_call(
        _rout_kernel,
        grid=(rows // tr,),
        in_specs=[
            pl.BlockSpec((tr, RET_QK), lambda i: (i, ZB_Q)),
            pl.BlockSpec((tr, RET_QK), lambda i: (i, ZB_K)),
            pl.BlockSpec((tr, RET_V), lambda i: (i, ZB2_V)),
            pl.BlockSpec((tr, RET_V), lambda i: (i, ZB2_GR)),
            state_spec,
            state_spec,
            pl.BlockSpec((RET_HEADS, 1, 1), lambda i: (0, 0, 0)),
            pl.BlockSpec((RET_HEADS, 1, 1), lambda i: (0, 0, 0)),
            pl.BlockSpec((1, RET_V), lambda i: (0, 0)),
        ],
        out_specs=pl.BlockSpec((tr, RET_V), lambda i: (i, 0)),
        out_shape=jax.ShapeDtypeStruct((rows, RET_V), BF16),
        compiler_params=_params(("arbitrary",)),
        name="rout",
    )(z, z, z, z, rf, rb, pf, pb, gn_g)


MERGE_TM = 512


def _merge_kernel(ca_ref, og_ref, sga_ref, sgb_ref, wc_ref, bc_ref, wr_ref, o_ref):
    for c in range(D_MODEL // MXU_N):
        cols = slice(c * MXU_N, (c + 1) * MXU_N)
        ya = _dot(ca_ref[...], wc_ref[:, cols]) + bc_ref[:, cols]
        yb = _dot(og_ref[...], wr_ref[:, cols])
        m = sga_ref[:, cols].astype(F32) * ya + sgb_ref[:, cols].astype(F32) * yb
        o_ref[:, cols] = m.astype(o_ref.dtype)


def _merge(cact, og, z, wc_b, bc, wr_b):
    rows = cact.shape[0]
    tm = MERGE_TM
    return pl.pallas_call(
        _merge_kernel,
        grid=(rows // tm,),
        in_specs=[
            pl.BlockSpec((tm, CONV_WIDTH), lambda i: (i, 0)),
            pl.BlockSpec((tm, RET_V), lambda i: (i, 0)),
            pl.BlockSpec((tm, D_MODEL), lambda i: (i, ZB2_SGA)),
            pl.BlockSpec((tm, D_MODEL), lambda i: (i, ZB2_SGB)),
            pl.BlockSpec((CONV_WIDTH, D_MODEL), lambda i: (0, 0)),
            pl.BlockSpec((1, D_MODEL), lambda i: (0, 0)),
            pl.BlockSpec((RET_V, D_MODEL), lambda i: (0, 0)),
        ],
        out_specs=pl.BlockSpec((tm, D_MODEL), lambda i: (i, 0)),
        out_shape=jax.ShapeDtypeStruct((rows, D_MODEL), BF16),
        compiler_params=_params(("arbitrary",)),
        name="merge",
    )(cact, og, z, z, wc_b, bc, wr_b)


OUTPROJ_TM = 512


def _outproj_kernel(m_ref, x_ref, wo_ref, gt_ref, g_ref, sc_ref, sh_ref, x1_ref, h2_ref):
    for c in range(D_MODEL // MXU_N):
        cols = slice(c * MXU_N, (c + 1) * MXU_N)
        mix = _dot(m_ref[...], wo_ref[:, cols])
        x1_ref[:, cols] = x_ref[:, cols] + gt_ref[:, cols] * mix
    h2_ref[...] = _modnorm_rows(x1_ref[...], g_ref[...], sc_ref[...], sh_ref[...]).astype(h2_ref.dtype)


def _outproj(merged, x2d, wo_b, mod3, norm2_g, seq):
    rows = merged.shape[0]
    tm = OUTPROJ_TM
    per_seq = seq // tm
    modspec = lambda blk: pl.BlockSpec((None, 1, D_MODEL), lambda i: (i // per_seq, 0, blk))
    return pl.pallas_call(
        _outproj_kernel,
        grid=(rows // tm,),
        in_specs=[
            pl.BlockSpec((tm, D_MODEL), lambda i: (i, 0)),
            pl.BlockSpec((tm, D_MODEL), lambda i: (i, 0)),
            pl.BlockSpec((D_MODEL, D_MODEL), lambda i: (0, 0)),
            modspec(2),
            pl.BlockSpec((1, D_MODEL), lambda i: (0, 0)),
            modspec(4),
            modspec(3),
        ],
        out_specs=[
            pl.BlockSpec((tm, D_MODEL), lambda i: (i, 0)),
            pl.BlockSpec((tm, D_MODEL), lambda i: (i, 0)),
        ],
        out_shape=[
            jax.ShapeDtypeStruct((rows, D_MODEL), F32),
            jax.ShapeDtypeStruct((rows, D_MODEL), BF16),
        ],
        compiler_params=_params(("arbitrary",)),
        name="outproj",
    )(merged, x2d, wo_b, mod3, norm2_g, mod3, mod3)


FFN_TM = 512
FFN_TF = 512


def _ffn_kernel(h_ref, x1_ref, wg_ref, wu_ref, wd_ref, gt_ref, fg_ref, o_ref):
    f = pl.program_id(1)
    acts = []
    for c in range(FFN_TF // MXU_N):
        cols = slice(c * MXU_N, (c + 1) * MXU_N)
        g = _dot(h_ref[...], wg_ref[:, cols])
        u = _dot(h_ref[...], wu_ref[:, cols])
        acts.append((_silu(g) * u).astype(BF16))
    act = jnp.concatenate(acts, axis=1)

    @pl.when(f == 0)
    def _first():
        for c in range(D_MODEL // MXU_N):
            cols = slice(c * MXU_N, (c + 1) * MXU_N)
            o_ref[:, cols] = _dot(act, wd_ref[:, cols])

    @pl.when(f > 0)
    def _accum():
        for c in range(D_MODEL // MXU_N):
            cols = slice(c * MXU_N, (c + 1) * MXU_N)
            o_ref[:, cols] += _dot(act, wd_ref[:, cols])

    @pl.when(f == pl.num_programs(1) - 1)
    def _final():
        x2 = x1_ref[...] + gt_ref[...] * o_ref[...]
        ms = jnp.mean(x2 * x2, axis=-1, keepdims=True)
        o_ref[...] = x2 * lax.rsqrt(ms + EPS) * fg_ref[...]


def _ffn(h2, x1, wg_b, wu_b, wd_b, mod3, final_g, seq):
    rows = h2.shape[0]
    tm, tf = FFN_TM, FFN_TF
    per_seq = seq // tm
    return pl.pallas_call(
        _ffn_kernel,
        grid=(rows // tm, D_FF // tf),
        in_specs=[
            pl.BlockSpec((tm, D_MODEL), lambda i, f: (i, 0)),
            pl.BlockSpec((tm, D_MODEL), lambda i, f: (i, 0)),
            pl.BlockSpec((D_MODEL, tf), lambda i, f: (0, f)),
            pl.BlockSpec((D_MODEL, tf), lambda i, f: (0, f + D_FF // tf)),
            pl.BlockSpec((tf, D_MODEL), lambda i, f: (f, 0)),
            pl.BlockSpec((None, 1, D_MODEL), lambda i, f: (i // per_seq, 0, 5)),
            pl.BlockSpec((1, D_MODEL), lambda i, f: (0, 0)),
        ],
        out_specs=pl.BlockSpec((tm, D_MODEL), lambda i, f: (i, 0)),
        out_shape=jax.ShapeDtypeStruct((rows, D_MODEL), F32),
        compiler_params=_params(("arbitrary", "arbitrary")),
        name="ffn",
    )(h2, x1, wg_b, wg_b, wd_b, mod3, final_g)


def _rope_tables(seq):
    half = RET_QK_DIM // 2
    inv = ROPE_THETA ** (-jnp.arange(half, dtype=F32) / half)
    ang = jnp.arange(seq, dtype=F32)[:, None] * inv[None, :]
    cos = jnp.cos(ang)
    sin = jnp.sin(ang)
    return jnp.concatenate([cos, cos], axis=1), jnp.concatenate([-sin, sin], axis=1)


def _encode(x, mod, w):
    batch, seq, _ = x.shape
    x2d = x.reshape(batch * seq, D_MODEL)
    mod3 = mod.reshape(batch, 1, N_MOD * D_MODEL)
    cos2, sin2 = _rope_tables(seq)
    h1 = _modnorm(x2d, w["norm1_g"], mod3, seq, shift_blk=0, scale_blk=1)
    z = _inproj(h1, w["w_in"], cos2, sin2, seq)
    cact = _conv(z, w["conv_w"], w["conv_b"], w["ln_g"], w["ln_b"], batch, seq)
    rf, rb = _rstate(z, w["pf"], w["pb"], batch, seq)
    og = _rout(z, rf, rb, w["pf"], w["pb"], w["gn_g"])
    merged = _merge(cact, og, z, w["w_conv_out"], w["b_conv_out"], w["w_ret_out"])
    x1, h2 = _outproj(merged, x2d, w["w_out"], mod3, w["norm2_g"], seq)
    y = _ffn(h2, x1, w["w_ffn_in"], w["w_ffn_in"], w["w_ffn_out"], mod3, w["final_g"], seq)
    return y.reshape(batch, seq, D_MODEL)


def kernel(x_prompt, x_sample, c_prompt, c_sample, norm1_g, norm2_g, w_ada, b_ada, w_in, conv_dw_w, conv_dw_b, conv_ln_g, conv_ln_b, w_conv_out, b_conv_out, ret_logdecay_fwd, ret_logdecay_bwd, ret_gn_g, w_ret_out, w_out, w_ffn_in, w_ffn_out, final_g):
    assert w_in.shape[0] == 1, "single-layer block"
    nb_p, nb_s = c_prompt.shape[0], c_sample.shape[0]
    pad = (-(nb_p + nb_s)) % 8
    c_all = jnp.concatenate([c_prompt, c_sample, jnp.zeros((pad, D_MODEL), F32)], axis=0)
    mod = _mod(c_all, w_ada[0], b_ada[0].reshape(1, -1))
    w = {
        "norm1_g": norm1_g[0].reshape(1, -1),
        "norm2_g": norm2_g[0].reshape(1, -1),
        "final_g": final_g.reshape(1, -1),
        "w_in": w_in[0].astype(BF16),
        "conv_w": conv_dw_w[0],
        "conv_b": conv_dw_b[0].reshape(1, -1),
        "ln_g": conv_ln_g[0].reshape(1, -1),
        "ln_b": conv_ln_b[0].reshape(1, -1),
        "w_conv_out": w_conv_out[0].astype(BF16),
        "b_conv_out": b_conv_out[0].reshape(1, -1),
        "pf": ret_logdecay_fwd[0].astype(F32).reshape(RET_HEADS, 1, 1),
        "pb": ret_logdecay_bwd[0].astype(F32).reshape(RET_HEADS, 1, 1),
        "gn_g": ret_gn_g[0].reshape(1, -1),
        "w_ret_out": w_ret_out[0].astype(BF16),
        "w_out": w_out[0].astype(BF16),
        "w_ffn_in": w_ffn_in[0].astype(BF16),
        "w_ffn_out": w_ffn_out[0].astype(BF16),
    }
    y_prompt = _encode(x_prompt, mod[:nb_p], w)
    y_sample = _encode(x_sample, mod[nb_p:nb_p + nb_s], w)
    return (y_prompt, y_sample)
```

```python
import functools

import jax
import jax.numpy as jnp
from jax import lax
from jax.experimental import pallas as pl
from jax.experimental.pallas import tpu as pltpu

F32 = jnp.float32
BF16 = jnp.bfloat16

D_MODEL = 2048
CONV_WIDTH = D_MODEL // 2
CONV_KERNEL = 31
RET_HEADS = 8
RET_QK_DIM = 128
RET_V_DIM = 256
RET_QK = RET_HEADS * RET_QK_DIM
RET_V = RET_HEADS * RET_V_DIM
CHUNK = 128
D_FF = 5632
ROPE_THETA = 10000.0
EPS = 1e-6
GN_EPS = 1e-5
N_MOD = 6

Z_COLS = 5 * 2048
ZB_Q, ZB_K = 0, 1
ZB2_V, ZB2_GR, ZB2_SGA, ZB2_SGB = 1, 2, 3, 4

MXU_N = 256
VMEM_LIMIT = 60 * 1024 * 1024
CONV_HALO = 16


def _params(sem):
    return pltpu.CompilerParams(dimension_semantics=sem, vmem_limit_bytes=VMEM_LIMIT)


def _sigmoid(x):
    return jax.nn.sigmoid(x)


def _silu(x):
    return x * jax.nn.sigmoid(x)


def _dot(a, b):
    return jnp.dot(a, b, preferred_element_type=F32)


def _mod_kernel(c_ref, w_ref, b_ref, o_ref):
    c = c_ref[...]
    s = _silu(c)
    o_ref[...] = _dot(s.astype(BF16), w_ref[...].astype(BF16)) + b_ref[...]


def _mod(c_all, w_ada, b_ada):
    rows = c_all.shape[0]
    tn = 1024
    return pl.pallas_call(
        _mod_kernel,
        grid=(N_MOD * D_MODEL // tn,),
        in_specs=[
            pl.BlockSpec((rows, D_MODEL), lambda j: (0, 0)),
            pl.BlockSpec((D_MODEL, tn), lambda j: (0, j)),
            pl.BlockSpec((1, tn), lambda j: (0, j)),
        ],
        out_specs=pl.BlockSpec((rows, tn), lambda j: (0, j)),
        out_shape=jax.ShapeDtypeStruct((rows, N_MOD * D_MODEL), F32),
        compiler_params=_params(("arbitrary",)),
        name="mod",
    )(c_all, w_ada, b_ada)


def _modnorm_rows(x, g, sc, sh):
    ms = jnp.mean(x * x, axis=-1, keepdims=True)
    y = x * lax.rsqrt(ms + EPS) * g
    return y * (1.0 + sc) + sh


def _modnorm_kernel(x_ref, g_ref, sc_ref, sh_ref, o_ref):
    o_ref[...] = _modnorm_rows(x_ref[...], g_ref[...], sc_ref[...], sh_ref[...]).astype(o_ref.dtype)


def _modnorm(x2d, g, mod3, seq, shift_blk, scale_blk):
    rows = x2d.shape[0]
    tm = 512
    per_seq = seq // tm
    return pl.pallas_call(
        _modnorm_kernel,
        grid=(rows // tm,),
        in_specs=[
            pl.BlockSpec((tm, D_MODEL), lambda i: (i, 0)),
            pl.BlockSpec((1, D_MODEL), lambda i: (0, 0)),
            pl.BlockSpec((None, 1, D_MODEL), lambda i: (i // per_seq, 0, scale_blk)),
            pl.BlockSpec((None, 1, D_MODEL), lambda i: (i // per_seq, 0, shift_blk)),
        ],
        out_specs=pl.BlockSpec((tm, D_MODEL), lambda i: (i, 0)),
        out_shape=jax.ShapeDtypeStruct((rows, D_MODEL), BF16),
        compiler_params=_params(("arbitrary",)),
        name="modnorm",
    )(x2d, g, mod3, mod3)


INPROJ_TM = 1024
INPROJ_TN = 2048
INPROJ_STEPS = 6


def _inproj_kernel(a_ref, w_ref, cos_ref, sin_ref, z_ref, u_ref):
    j = pl.program_id(1)
    nchunks = INPROJ_TN // MXU_N

    def z_chunk(c):
        return _dot(a_ref[...], w_ref[:, c * MXU_N:(c + 1) * MXU_N])

    def store(c, val):
        z_ref[:, c * MXU_N:(c + 1) * MXU_N] = val.astype(z_ref.dtype)

    @pl.when(j == 0)
    def _glu():
        half = nchunks // 2
        for c in range(half):
            u = z_chunk(c) * _sigmoid(z_chunk(c + half))
            u_ref[:, c * MXU_N:(c + 1) * MXU_N] = u.astype(u_ref.dtype)

    @pl.when(j == 1)
    def _rotary():
        cos2 = cos_ref[...]
        sin2 = sin_ref[...]
        for c in range(nchunks):
            z = z_chunk(c)
            is_k = c * MXU_N >= RET_QK
            outs = []
            for hh in range(MXU_N // RET_QK_DIM):
                zh = z[:, hh * RET_QK_DIM:(hh + 1) * RET_QK_DIM]
                rot = pltpu.roll(zh, RET_QK_DIM // 2, 1)
                r = zh * cos2 + rot * sin2
                outs.append(r * (RET_QK_DIM ** -0.5) if is_k else r)
            store(c, jnp.concatenate(outs, axis=1))

    @pl.when(j == 2)
    def _identity():
        for c in range(nchunks):
            store(c, z_chunk(c))

    @pl.when(j == 3)
    def _swish():
        for c in range(nchunks):
            store(c, _silu(z_chunk(c)))

    @pl.when(j >= 4)
    def _gate():
        for c in range(nchunks):
            store(c, _sigmoid(z_chunk(c)))


def _inproj(h1, w_in_b, cos2, sin2, seq):
    rows = h1.shape[0]
    tm, tn = INPROJ_TM, INPROJ_TN
    per_seq = seq // tm
    return pl.pallas_call(
        _inproj_kernel,
        grid=(rows // tm, INPROJ_STEPS),
        in_specs=[
            pl.BlockSpec((tm, D_MODEL), lambda i, j: (i, 0)),
            pl.BlockSpec((D_MODEL, tn), lambda i, j: (0, j)),
            pl.BlockSpec((tm, RET_QK_DIM), lambda i, j: (i % per_seq, 0)),
            pl.BlockSpec((tm, RET_QK_DIM), lambda i, j: (i % per_seq, 0)),
        ],
        out_specs=[
            pl.BlockSpec((tm, tn), lambda i, j: (i, jnp.maximum(j - 1, 0))),
            pl.BlockSpec((tm, CONV_WIDTH), lambda i, j: (i, 0)),
        ],
        out_shape=[
            jax.ShapeDtypeStruct((rows, Z_COLS), BF16),
            jax.ShapeDtypeStruct((rows, CONV_WIDTH), BF16),
        ],
        compiler_params=_params(("arbitrary", "arbitrary")),
        name="inproj",
    )(h1, w_in_b, cos2, sin2)


CONV_TM = 512
CONV_SUB = 8
CONV_LANES = 128
CONV_S = CONV_TM // CONV_SUB
CONV_PIN = CONV_S + 2 * CONV_HALO + 4
CONV_POUT = CONV_S + 4
CONV_RB = 16
CONV_LN_UNROLL = 16
CONV_SLABS = CONV_WIDTH // CONV_LANES
assert (CONV_PIN // 4) % 2 == 1 and (CONV_POUT // 4) % 2 == 1 and CONV_HALO >= CONV_KERNEL // 2


def _conv_kernel(up_ref, um_ref, un_ref, w_ref, b_ref, lg_ref, lb_ref, o_ref, pbuf, ybuf, obuf):
    i = pl.program_id(1)
    last = pl.num_programs(1) - 1
    S, H, PIN, POUT = CONV_S, CONV_HALO, CONV_PIN, CONV_POUT
    prev = jnp.where(i > 0, up_ref[...].astype(F32), 0.0)
    nxt = jnp.where(i < last, un_ref[...].astype(F32), 0.0)

    for s in range(CONV_SUB):
        for l in range(CONV_SLABS):
            lanes = slice(l * CONV_LANES, (l + 1) * CONV_LANES)
            left = prev[:, lanes] if s == 0 else um_ref[s * S - H:s * S, lanes].astype(F32)
            right = nxt[:, lanes] if s == CONV_SUB - 1 else um_ref[(s + 1) * S:(s + 1) * S + H, lanes].astype(F32)
            pbuf[l, s * PIN:s * PIN + H, :] = left
            pbuf[l, s * PIN + H:s * PIN + H + S, :] = um_ref[s * S:(s + 1) * S, lanes].astype(F32)
            pbuf[l, s * PIN + H + S:s * PIN + 2 * H + S, :] = right

    off = H - CONV_KERNEL // 2
    for l in range(CONV_SLABS):
        lanes = slice(l * CONV_LANES, (l + 1) * CONV_LANES)
        wv = [jnp.broadcast_to(w_ref[k:k + 1, lanes], (CONV_SUB, CONV_LANES)) for k in range(CONV_KERNEL)]
        bias = jnp.broadcast_to(b_ref[:, lanes], (CONV_SUB, CONV_LANES))

        def conv_body(jb, carry, l=l, wv=wv, bias=bias):
            j0 = jb * CONV_RB
            accs = [bias] * CONV_RB
            for mm in range(CONV_RB + CONV_KERNEL - 1):
                g = pbuf[l, pl.ds(j0 + off + mm, CONV_SUB, stride=PIN), :]
                for jj in range(CONV_RB):
                    k = mm - jj
                    if 0 <= k < CONV_KERNEL:
                        accs[jj] = accs[jj] + g * wv[k]
            for jj in range(CONV_RB):
                ybuf[l, pl.ds((j0 + jj) * CONV_SUB, CONV_SUB), :] = accs[jj]
            return carry

        lax.fori_loop(0, S // CONV_RB, conv_body, 0)

    lgs = [jnp.broadcast_to(lg_ref[:, l * CONV_LANES:(l + 1) * CONV_LANES], (CONV_SUB, CONV_LANES))
           for l in range(CONV_SLABS)]
    lbs = [jnp.broadcast_to(lb_ref[:, l * CONV_LANES:(l + 1) * CONV_LANES], (CONV_SUB, CONV_LANES))
           for l in range(CONV_SLABS)]

    def ln_body(j, carry):
        r0 = pl.multiple_of(j * CONV_SUB, CONV_SUB)
        ys = [ybuf[l, pl.ds(r0, CONV_SUB), :] for l in range(CONV_SLABS)]
        tot = ys[0]
        for y in ys[1:]:
            tot = tot + y
        mu = jnp.sum(tot, axis=-1, keepdims=True) * (1.0 / CONV_WIDTH)
        cs = [y - mu for y in ys]
        sq = cs[0] * cs[0]
        for c in cs[1:]:
            sq = sq + c * c
        var = jnp.sum(sq, axis=-1, keepdims=True) * (1.0 / CONV_WIDTH)
        inv = lax.rsqrt(var + GN_EPS)
        for l in range(CONV_SLABS):
            yn = cs[l] * inv * lgs[l] + lbs[l]
            obuf[l, pl.ds(j, CONV_SUB, stride=POUT), :] = _silu(yn)
        return carry

    lax.fori_loop(0, S, ln_body, 0, unroll=CONV_LN_UNROLL)

    for s in range(CONV_SUB):
        for l in range(CONV_SLABS):
            o_ref[s * S:(s + 1) * S, l * CONV_LANES:(l + 1) * CONV_LANES] = (
                obuf[l, s * POUT:s * POUT + S, :].astype(o_ref.dtype))


def _conv(u, conv_w, conv_b, ln_g, ln_b, batch, seq):
    tm = CONV_TM
    per_seq = seq // tm
    hb = tm // CONV_HALO
    nhb = seq // CONV_HALO

    def prev_map(b, i):
        return (b * nhb + jnp.maximum(i * hb - 1, 0), 0)

    def next_map(b, i):
        return (b * nhb + jnp.minimum((i + 1) * hb, nhb - 1), 0)

    return pl.pallas_call(
        _conv_kernel,
        grid=(batch, per_seq),
        in_specs=[
            pl.BlockSpec((CONV_HALO, CONV_WIDTH), prev_map),
            pl.BlockSpec((tm, CONV_WIDTH), lambda b, i: (b * per_seq + i, 0)),
            pl.BlockSpec((CONV_HALO, CONV_WIDTH), next_map),
            pl.BlockSpec((CONV_KERNEL, CONV_WIDTH), lambda b, i: (0, 0)),
            pl.BlockSpec((1, CONV_WIDTH), lambda b, i: (0, 0)),
            pl.BlockSpec((1, CONV_WIDTH), lambda b, i: (0, 0)),
            pl.BlockSpec((1, CONV_WIDTH), lambda b, i: (0, 0)),
        ],
        out_specs=pl.BlockSpec((tm, CONV_WIDTH), lambda b, i: (b * per_seq + i, 0)),
        out_shape=jax.ShapeDtypeStruct((batch * seq, CONV_WIDTH), BF16),
        scratch_shapes=[
            pltpu.VMEM((CONV_SLABS, CONV_SUB * CONV_PIN, CONV_LANES), F32),
            pltpu.VMEM((CONV_SLABS, CONV_TM, CONV_LANES), F32),
            pltpu.VMEM((CONV_SLABS, CONV_SUB * CONV_POUT, CONV_LANES), F32),
        ],
        compiler_params=_params(("arbitrary", "arbitrary")),
        name="conv",
    )(u, u, u, conv_w, conv_b, ln_g, ln_b)


RS_CS = 4


def _log_gamma(p):
    return jnp.log1p(-jnp.exp2(p))


def _rstate_kernel(kf_ref, vf_ref, kb_ref, vb_ref, pf_ref, pb_ref, rf_ref, rb_ref, sf_ref, sb_ref):
    n = pl.program_id(1)

    @pl.when(n == 0)
    def _init():
        sf_ref[...] = jnp.zeros_like(sf_ref)
        sb_ref[...] = jnp.zeros_like(sb_ref)

    row = lax.broadcasted_iota(jnp.int32, (CHUNK, RET_QK_DIM), 0).astype(F32)
    tdims = (((0,), (0,)), ((), ()))

    def scan(k_ref, v_ref, r_ref, s_ref, p_ref, order, fwd):
        for h in range(RET_HEADS):
            lg = _log_gamma(p_ref[h])
            pos = (CHUNK - 1.0 - row) if fwd else row
            kdec = jnp.exp(pos * lg)
            cdec = jnp.exp(CHUNK * lg)
            state = s_ref[h]
            for c in order:
                rows = slice(c * CHUNK, (c + 1) * CHUNK)
                r_ref[c, h] = state.astype(r_ref.dtype)
                kh = k_ref[rows, h * RET_QK_DIM:(h + 1) * RET_QK_DIM].astype(F32)
                kd = (kh * kdec).astype(BF16)
                vh = v_ref[rows, h * RET_V_DIM:(h + 1) * RET_V_DIM]
                kv = lax.dot_general(kd, vh, tdims, preferred_element_type=F32)
                state = state * cdec + kv
            s_ref[h] = state

    scan(kf_ref, vf_ref, rf_ref, sf_ref, pf_ref, tuple(range(RS_CS)), True)
    scan(kb_ref, vb_ref, rb_ref, sb_ref, pb_ref, tuple(reversed(range(RS_CS))), False)


def _rstate(z, pf, pb, batch, seq):
    tr = RS_CS * CHUNK
    ns = seq // tr
    nchunks = batch * seq // CHUNK
    state = jax.ShapeDtypeStruct((nchunks, RET_HEADS, RET_QK_DIM, RET_V_DIM), BF16)
    fmap = lambda b, n: (b * ns + n)
    bmap = lambda b, n: (b * ns + ns - 1 - n)
    return pl.pallas_call(
        _rstate_kernel,
        grid=(batch, ns),
        in_specs=[
            pl.BlockSpec((tr, RET_QK), lambda b, n: (fmap(b, n), ZB_K)),
            pl.BlockSpec((tr, RET_V), lambda b, n: (fmap(b, n), ZB2_V)),
            pl.BlockSpec((tr, RET_QK), lambda b, n: (bmap(b, n), ZB_K)),
            pl.BlockSpec((tr, RET_V), lambda b, n: (bmap(b, n), ZB2_V)),
            pl.BlockSpec((RET_HEADS, 1, 1), lambda b, n: (0, 0, 0)),
            pl.BlockSpec((RET_HEADS, 1, 1), lambda b, n: (0, 0, 0)),
        ],
        out_specs=[
            pl.BlockSpec((RS_CS, RET_HEADS, RET_QK_DIM, RET_V_DIM), lambda b, n: (fmap(b, n), 0, 0, 0)),
            pl.BlockSpec((RS_CS, RET_HEADS, RET_QK_DIM, RET_V_DIM), lambda b, n: (bmap(b, n), 0, 0, 0)),
        ],
        out_shape=[state, state],
        scratch_shapes=[
            pltpu.VMEM((RET_HEADS, RET_QK_DIM, RET_V_DIM), F32),
            pltpu.VMEM((RET_HEADS, RET_QK_DIM, RET_V_DIM), F32),
        ],
        compiler_params=_params(("arbitrary", "arbitrary")),
        name="rstate",
    )(z, z, z, z, pf, pb)


RO_CS = 4


def _rout_kernel(q_ref, k_ref, v_ref, gr_ref, rf_ref, rb_ref, pf_ref, pb_ref, gn_ref, o_ref):
    ri = lax.broadcasted_iota(jnp.int32, (CHUNK, CHUNK), 0)
    ci = lax.broadcasted_iota(jnp.int32, (CHUNK, CHUNK), 1)
    diff = (ri - ci).astype(F32)
    row = ri.astype(F32)
    ndims = (((1,), (1,)), ((), ()))
    for h in range(RET_HEADS):
        lgf = _log_gamma(pf_ref[h])
        lgb = _log_gamma(pb_ref[h])
        lower = diff >= 0
        upper = diff <= 0
        mask = (jnp.where(lower, jnp.exp(jnp.where(lower, diff, 0.0) * lgf), 0.0)
                + jnp.where(upper, jnp.exp(jnp.where(upper, -diff, 0.0) * lgb), 0.0))
        qdec_f = jnp.exp((row + 1.0) * lgf)
        qdec_b = jnp.exp((CHUNK - row) * lgb)
        gn = gn_ref[:, h * RET_V_DIM:(h + 1) * RET_V_DIM]
        for c in range(RO_CS):
            rows = slice(c * CHUNK, (c + 1) * CHUNK)
            q = q_ref[rows, h * RET_QK_DIM:(h + 1) * RET_QK_DIM]
            k = k_ref[rows, h * RET_QK_DIM:(h + 1) * RET_QK_DIM]
            v = v_ref[rows, h * RET_V_DIM:(h + 1) * RET_V_DIM]
            s = lax.dot_general(q, k, ndims, preferred_element_type=F32)
            inner = _dot((s * mask).astype(BF16), v)
            qf = q.astype(F32)
            qd = jnp.concatenate([(qf * qdec_f).astype(BF16), (qf * qdec_b).astype(BF16)], axis=1)
            rcat = jnp.concatenate([rf_ref[c, h], rb_ref[c, h]], axis=0)
            o = inner + _dot(qd, rcat)
            mu = jnp.mean(o, axis=-1, keepdims=True)
            cen = o - mu
            var = jnp.mean(cen * cen, axis=-1, keepdims=True)
            on = cen * lax.rsqrt(var + GN_EPS) * gn
            gr = gr_ref[rows, h * RET_V_DIM:(h + 1) * RET_V_DIM].astype(F32)
            o_ref[rows, h * RET_V_DIM:(h + 1) * RET_V_DIM] = (on * gr).astype(o_ref.dtype)


def _rout(z, rf, rb, pf, pb, gn_g):
    rows = z.shape[0]
    tr = RO_CS * CHUNK
    state_spec = pl.BlockSpec((RO_CS, RET_HEADS, RET_QK_DIM, RET_V_DIM), lambda i: (i, 0, 0, 0))
    return pl.pallas_call(
        _rout_kernel,
        grid=(rows // tr,),
        in_specs=[
            pl.BlockSpec((tr, RET_QK), lambda i: (i, ZB_Q)),
            pl.BlockSpec((tr, RET_QK), lambda i: (i, ZB_K)),
            pl.BlockSpec((tr, RET_V), lambda i: (i, ZB2_V)),
            pl.BlockSpec((tr, RET_V), lambda i: (i, ZB2_GR)),
            state_spec,
            state_spec,
            pl.BlockSpec((RET_HEADS, 1, 1), lambda i: (0, 0, 0)),
            pl.BlockSpec((RET_HEADS, 1, 1), lambda i: (0, 0, 0)),
            pl.BlockSpec((1, RET_V), lambda i: (0, 0)),
        ],
        out_specs=pl.BlockSpec((tr, RET_V), lambda i: (i, 0)),
        out_shape=jax.ShapeDtypeStruct((rows, RET_V), BF16),
        compiler_params=_params(("arbitrary",)),
        name="rout",
    )(z, z, z, z, rf, rb, pf, pb, gn_g)


MERGE_TM = 512


def _merge_kernel(ca_ref, og_ref, sga_ref, sgb_ref, wc_ref, bc_ref, wr_ref, o_ref):
    for c in range(D_MODEL // MXU_N):
        cols = slice(c * MXU_N, (c + 1) * MXU_N)
        ya = _dot(ca_ref[...], wc_ref[:, cols]) + bc_ref[:, cols]
        yb = _dot(og_ref[...], wr_ref[:, cols])
        m = sga_ref[:, cols].astype(F32) * ya + sgb_ref[:, cols].astype(F32) * yb
        o_ref[:, cols] = m.astype(o_ref.dtype)


def _merge(cact, og, z, wc_b, bc, wr_b):
    rows = cact.shape[0]
    tm = MERGE_TM
    return pl.pallas_call(
        _merge_kernel,
        grid=(rows // tm,),
        in_specs=[
            pl.BlockSpec((tm, CONV_WIDTH), lambda i: (i, 0)),
            pl.BlockSpec((tm, RET_V), lambda i: (i, 0)),
            pl.BlockSpec((tm, D_MODEL), lambda i: (i, ZB2_SGA)),
            pl.BlockSpec((tm, D_MODEL), lambda i: (i, ZB2_SGB)),
            pl.BlockSpec((CONV_WIDTH, D_MODEL), lambda i: (0, 0)),
            pl.BlockSpec((1, D_MODEL), lambda i: (0, 0)),
            pl.BlockSpec((RET_V, D_MODEL), lambda i: (0, 0)),
        ],
        out_specs=pl.BlockSpec((tm, D_MODEL), lambda i: (i, 0)),
        out_shape=jax.ShapeDtypeStruct((rows, D_MODEL), BF16),
        compiler_params=_params(("arbitrary",)),
        name="merge",
    )(cact, og, z, z, wc_b, bc, wr_b)


OUTPROJ_TM = 512


def _outproj_kernel(m_ref, x_ref, wo_ref, gt_ref, g_ref, sc_ref, sh_ref, x1_ref, h2_ref):
    for c in range(D_MODEL // MXU_N):
        cols = slice(c * MXU_N, (c + 1) * MXU_N)
        mix = _dot(m_ref[...], wo_ref[:, cols])
        x1_ref[:, cols] = x_ref[:, cols] + gt_ref[:, cols] * mix
    h2_ref[...] = _modnorm_rows(x1_ref[...], g_ref[...], sc_ref[...], sh_ref[...]).astype(h2_ref.dtype)


def _outproj(merged, x2d, wo_b, mod3, norm2_g, seq):
    rows = merged.shape[0]
    tm = OUTPROJ_TM
    per_seq = seq // tm
    modspec = lambda blk: pl.BlockSpec((None, 1, D_MODEL), lambda i: (i // per_seq, 0, blk))
    return pl.pallas_call(
        _outproj_kernel,
        grid=(rows // tm,),
        in_specs=[
            pl.BlockSpec((tm, D_MODEL), lambda i: (i, 0)),
            pl.BlockSpec((tm, D_MODEL), lambda i: (i, 0)),
            pl.BlockSpec((D_MODEL, D_MODEL), lambda i: (0, 0)),
            modspec(2),
            pl.BlockSpec((1, D_MODEL), lambda i: (0, 0)),
            modspec(4),
            modspec(3),
        ],
        out_specs=[
            pl.BlockSpec((tm, D_MODEL), lambda i: (i, 0)),
            pl.BlockSpec((tm, D_MODEL), lambda i: (i, 0)),
        ],
        out_shape=[
            jax.ShapeDtypeStruct((rows, D_MODEL), F32),
            jax.ShapeDtypeStruct((rows, D_MODEL), BF16),
        ],
        compiler_params=_params(("arbitrary",)),
        name="outproj",
    )(merged, x2d, wo_b, mod3, norm2_g, mod3, mod3)


FFN_TM = 1024
FFN_TF = 512


def _ffn_kernel(h_ref, x1_hbm, wg_ref, wu_ref, wd_ref, gt_ref, fg_ref, o_ref, x1_buf, x1_sem):
    i = pl.program_id(0)
    f = pl.program_id(1)

    def x1_copy():
        return pltpu.make_async_copy(x1_hbm.at[pl.ds(i * FFN_TM, FFN_TM), :], x1_buf, x1_sem)

    @pl.when(f == 0)
    def _fetch_residual():
        x1_copy().start()

    acts = []
    for c in range(FFN_TF // MXU_N):
        cols = slice(c * MXU_N, (c + 1) * MXU_N)
        g = _dot(h_ref[...], wg_ref[:, cols])
        u = _dot(h_ref[...], wu_ref[:, cols])
        acts.append((_silu(g) * u).astype(BF16))
    act = jnp.concatenate(acts, axis=1)

    @pl.when(f == 0)
    def _first():
        for c in range(D_MODEL // MXU_N):
            cols = slice(c * MXU_N, (c + 1) * MXU_N)
            o_ref[:, cols] = _dot(act, wd_ref[:, cols])

    @pl.when(f > 0)
    def _accum():
        for c in range(D_MODEL // MXU_N):
            cols = slice(c * MXU_N, (c + 1) * MXU_N)
            o_ref[:, cols] += _dot(act, wd_ref[:, cols])

    @pl.when(f == pl.num_programs(1) - 1)
    def _final():
        x1_copy().wait()
        x2 = x1_buf[...] + gt_ref[...] * o_ref[...]
        ms = jnp.mean(x2 * x2, axis=-1, keepdims=True)
        o_ref[...] = x2 * lax.rsqrt(ms + EPS) * fg_ref[...]


def _ffn(h2, x1, wg_b, wu_b, wd_b, mod3, final_g, seq):
    rows = h2.shape[0]
    tm, tf = FFN_TM, FFN_TF
    per_seq = seq // tm
    return pl.pallas_call(
        _ffn_kernel,
        grid=(rows // tm, D_FF // tf),
        in_specs=[
            pl.BlockSpec((tm, D_MODEL), lambda i, f: (i, 0)),
            pl.BlockSpec(memory_space=pl.ANY),
            pl.BlockSpec((D_MODEL, tf), lambda i, f: (0, f)),
            pl.BlockSpec((D_MODEL, tf), lambda i, f: (0, f + D_FF // tf)),
            pl.BlockSpec((tf, D_MODEL), lambda i, f: (f, 0)),
            pl.BlockSpec((None, 1, D_MODEL), lambda i, f: (i // per_seq, 0, 5)),
            pl.BlockSpec((1, D_MODEL), lambda i, f: (0, 0)),
        ],
        out_specs=pl.BlockSpec((tm, D_MODEL), lambda i, f: (i, 0)),
        out_shape=jax.ShapeDtypeStruct((rows, D_MODEL), F32),
        scratch_shapes=[pltpu.VMEM((tm, D_MODEL), F32), pltpu.SemaphoreType.DMA(())],
        compiler_params=_params(("arbitrary", "arbitrary")),
        name="ffn",
    )(h2, x1, wg_b, wg_b, wd_b, mod3, final_g)


def _rope_tables(seq):
    half = RET_QK_DIM // 2
    inv = ROPE_THETA ** (-jnp.arange(half, dtype=F32) / half)
    ang = jnp.arange(seq, dtype=F32)[:, None] * inv[None, :]
    cos = jnp.cos(ang)
    sin = jnp.sin(ang)
    return jnp.concatenate([cos, cos], axis=1), jnp.concatenate([-sin, sin], axis=1)


def _encode(x, mod, w):
    batch, seq, _ = x.shape
    x2d = x.reshape(batch * seq, D_MODEL)
    mod3 = mod.reshape(batch, 1, N_MOD * D_MODEL)
    cos2, sin2 = _rope_tables(seq)
    h1 = _modnorm(x2d, w["norm1_g"], mod3, seq, shift_blk=0, scale_blk=1)
    z, u = _inproj(h1, w["w_in"], cos2, sin2, seq)
    cact = _conv(u, w["conv_w"], w["conv_b"], w["ln_g"], w["ln_b"], batch, seq)
    rf, rb = _rstate(z, w["pf"], w["pb"], batch, seq)
    og = _rout(z, rf, rb, w["pf"], w["pb"], w["gn_g"])
    merged = _merge(cact, og, z, w["w_conv_out"], w["b_conv_out"], w["w_ret_out"])
    x1, h2 = _outproj(merged, x2d, w["w_out"], mod3, w["norm2_g"], seq)
    y = _ffn(h2, x1, w["w_ffn_in"], w["w_ffn_in"], w["w_ffn_out"], mod3, w["final_g"], seq)
    return y.reshape(batch, seq, D_MODEL)


def kernel(x_prompt, x_sample, c_prompt, c_sample, norm1_g, norm2_g, w_ada, b_ada, w_in, conv_dw_w, conv_dw_b, conv_ln_g, conv_ln_b, w_conv_out, b_conv_out, ret_logdecay_fwd, ret_logdecay_bwd, ret_gn_g, w_ret_out, w_out, w_ffn_in, w_ffn_out, final_g):
    assert w_in.shape[0] == 1, "single-layer block"
    nb_p, nb_s = c_prompt.shape[0], c_sample.shape[0]
    pad = (-(nb_p + nb_s)) % 8
    c_all = jnp.concatenate([c_prompt, c_sample, jnp.zeros((pad, D_MODEL), F32)], axis=0)
    mod = _mod(c_all, w_ada[0], b_ada[0].reshape(1, -1))
    w = {
        "norm1_g": norm1_g[0].reshape(1, -1),
        "norm2_g": norm2_g[0].reshape(1, -1),
        "final_g": final_g.reshape(1, -1),
        "w_in": w_in[0].astype(BF16),
        "conv_w": conv_dw_w[0],
        "conv_b": conv_dw_b[0].reshape(1, -1),
        "ln_g": conv_ln_g[0].reshape(1, -1),
        "ln_b": conv_ln_b[0].reshape(1, -1),
        "w_conv_out": w_conv_out[0].astype(BF16),
        "b_conv_out": b_conv_out[0].reshape(1, -1),
        "pf": ret_logdecay_fwd[0].astype(F32).reshape(RET_HEADS, 1, 1),
        "pb": ret_logdecay_bwd[0].astype(F32).reshape(RET_HEADS, 1, 1),
        "gn_g": ret_gn_g[0].reshape(1, -1),
        "w_ret_out": w_ret_out[0].astype(BF16),
        "w_out": w_out[0].astype(BF16),
        "w_ffn_in": w_ffn_in[0].astype(BF16),
        "w_ffn_out": w_ffn_out[0].astype(BF16),
    }
    y_prompt = _encode(x_prompt, mod[:nb_p], w)
    y_sample = _encode(x_sample, mod[nb_p:nb_p + nb_s], w)
    return (y_prompt, y_sample)
```

```python
import jax
import jax.numpy as jnp
from jax import lax
from jax.experimental import pallas as pl
from jax.experimental.pallas import tpu as pltpu

F32 = jnp.float32
BF16 = jnp.bfloat16

D_MODEL = 2048
CONV_WIDTH = D_MODEL // 2
CONV_KERNEL = 31
RET_HEADS = 8
RET_QK_DIM = 128
RET_V_DIM = 256
RET_QK = RET_HEADS * RET_QK_DIM
RET_V = RET_HEADS * RET_V_DIM
CHUNK = 128
D_FF = 5632
ROPE_THETA = 10000.0
EPS = 1e-6
GN_EPS = 1e-5
N_MOD = 6

Z_COLS = 5 * 2048
ZB_Q, ZB_K = 0, 1
ZB2_V, ZB2_GR, ZB2_SGA, ZB2_SGB = 1, 2, 3, 4

MXU_N = 256
VMEM_LIMIT = 60 * 1024 * 1024
CONV_HALO = 16


def _params(sem):
    return pltpu.CompilerParams(dimension_semantics=sem, vmem_limit_bytes=VMEM_LIMIT)


def _sigmoid(x):
    return jax.nn.sigmoid(x)


def _silu(x):
    return x * jax.nn.sigmoid(x)


def _dot(a, b):
    return jnp.dot(a, b, preferred_element_type=F32)


def _modnorm_rows(x, g, sc, sh):
    ms = jnp.mean(x * x, axis=-1, keepdims=True)
    y = x * lax.rsqrt(ms + EPS) * g
    return y * (1.0 + sc) + sh


def _mod_kernel(c_ref, w_ref, b_ref, o_ref):
    c = c_ref[...]
    s = _silu(c)
    o_ref[...] = _dot(s.astype(BF16), w_ref[...].astype(BF16)) + b_ref[...]


def _mod(c_all, w_ada, b_ada):
    rows = c_all.shape[0]
    tn = 1024
    return pl.pallas_call(
        _mod_kernel,
        grid=(N_MOD * D_MODEL // tn,),
        in_specs=[
            pl.BlockSpec((rows, D_MODEL), lambda j: (0, 0)),
            pl.BlockSpec((D_MODEL, tn), lambda j: (0, j)),
            pl.BlockSpec((1, tn), lambda j: (0, j)),
        ],
        out_specs=pl.BlockSpec((rows, tn), lambda j: (0, j)),
        out_shape=jax.ShapeDtypeStruct((rows, N_MOD * D_MODEL), F32),
        compiler_params=_params(("arbitrary",)),
        name="mod",
    )(c_all, w_ada, b_ada)


INPROJ_TM = 1024
INPROJ_TN = 2048
INPROJ_STEPS = 6
INPROJ_NORM_ROWS = 128


def _inproj_kernel(x_hbm, g_ref, sc_ref, sh_ref, w_ref, cos_ref, sin_ref, z_ref, u_ref, x_buf, h_buf, x_sem):
    i = pl.program_id(0)
    j = pl.program_id(1)
    nchunks = INPROJ_TN // MXU_N

    def x_copy(tile):
        return pltpu.make_async_copy(x_hbm.at[pl.ds(tile * INPROJ_TM, INPROJ_TM), :], x_buf, x_sem)

    @pl.when(j == 0)
    def _modnorm():
        @pl.when(i == 0)
        def _first_fetch():
            x_copy(0).start()

        x_copy(i).wait()
        for r in range(0, INPROJ_TM, INPROJ_NORM_ROWS):
            rows = slice(r, r + INPROJ_NORM_ROWS)
            h_buf[rows, :] = _modnorm_rows(x_buf[rows, :], g_ref[...], sc_ref[...], sh_ref[...]).astype(h_buf.dtype)

        @pl.when(i + 1 < pl.num_programs(0))
        def _next_fetch():
            x_copy(i + 1).start()

    def z_chunk(c):
        return _dot(h_buf[...], w_ref[:, c * MXU_N:(c + 1) * MXU_N])

    def store(c, val):
        z_ref[:, c * MXU_N:(c + 1) * MXU_N] = val.astype(z_ref.dtype)

    @pl.when(j == 0)
    def _glu():
        half = nchunks // 2
        for c in range(half):
            u = z_chunk(c) * _sigmoid(z_chunk(c + half))
            u_ref[:, c * MXU_N:(c + 1) * MXU_N] = u.astype(u_ref.dtype)

    @pl.when(j == 1)
    def _rotary():
        cos2 = cos_ref[...]
        sin2 = sin_ref[...]
        for c in range(nchunks):
            z = z_chunk(c)
            is_k = c * MXU_N >= RET_QK
            outs = []
            for hh in range(MXU_N // RET_QK_DIM):
                zh = z[:, hh * RET_QK_DIM:(hh + 1) * RET_QK_DIM]
                rot = pltpu.roll(zh, RET_QK_DIM // 2, 1)
                r = zh * cos2 + rot * sin2
                outs.append(r * (RET_QK_DIM ** -0.5) if is_k else r)
            store(c, jnp.concatenate(outs, axis=1))

    @pl.when(j == 2)
    def _identity():
        for c in range(nchunks):
            store(c, z_chunk(c))

    @pl.when(j == 3)
    def _swish():
        for c in range(nchunks):
            store(c, _silu(z_chunk(c)))

    @pl.when(j >= 4)
    def _gate():
        for c in range(nchunks):
            store(c, _sigmoid(z_chunk(c)))


def _inproj(x2d, norm_g, mod3, w_in_b, cos2, sin2, seq):
    rows = x2d.shape[0]
    tm, tn = INPROJ_TM, INPROJ_TN
    per_seq = seq // tm
    modspec = lambda blk: pl.BlockSpec((None, 1, D_MODEL), lambda i, j: (i // per_seq, 0, blk))
    return pl.pallas_call(
        _inproj_kernel,
        grid=(rows // tm, INPROJ_STEPS),
        in_specs=[
            pl.BlockSpec(memory_space=pl.ANY),
            pl.BlockSpec((1, D_MODEL), lambda i, j: (0, 0)),
            modspec(1),
            modspec(0),
            pl.BlockSpec((D_MODEL, tn), lambda i, j: (0, j)),
            pl.BlockSpec((tm, RET_QK_DIM), lambda i, j: (i % per_seq, 0)),
            pl.BlockSpec((tm, RET_QK_DIM), lambda i, j: (i % per_seq, 0)),
        ],
        out_specs=[
            pl.BlockSpec((tm, tn), lambda i, j: (i, jnp.maximum(j - 1, 0))),
            pl.BlockSpec((tm, CONV_WIDTH), lambda i, j: (i, 0)),
        ],
        out_shape=[
            jax.ShapeDtypeStruct((rows, Z_COLS), BF16),
            jax.ShapeDtypeStruct((rows, CONV_WIDTH), BF16),
        ],
        scratch_shapes=[
            pltpu.VMEM((tm, D_MODEL), F32),
            pltpu.VMEM((tm, D_MODEL), BF16),
            pltpu.SemaphoreType.DMA(()),
        ],
        compiler_params=_params(("arbitrary", "arbitrary")),
        name="inproj",
    )(x2d, norm_g, mod3, mod3, w_in_b, cos2, sin2)


CONV_TM = 512
CONV_SUB = 8
CONV_LANES = 128
CONV_S = CONV_TM // CONV_SUB
CONV_PIN = CONV_S + 2 * CONV_HALO + 4
CONV_POUT = CONV_S + 4
CONV_RB = 16
CONV_LN_UNROLL = 16
CONV_SLABS = CONV_WIDTH // CONV_LANES
assert (CONV_PIN // 4) % 2 == 1 and (CONV_POUT // 4) % 2 == 1 and CONV_HALO >= CONV_KERNEL // 2


def _conv_kernel(up_ref, um_ref, un_ref, w_ref, b_ref, lg_ref, lb_ref, o_ref, pbuf, ybuf, obuf):
    i = pl.program_id(1)
    last = pl.num_programs(1) - 1
    S, H, PIN, POUT = CONV_S, CONV_HALO, CONV_PIN, CONV_POUT
    prev = jnp.where(i > 0, up_ref[...].astype(F32), 0.0)
    nxt = jnp.where(i < last, un_ref[...].astype(F32), 0.0)

    for s in range(CONV_SUB):
        for l in range(CONV_SLABS):
            lanes = slice(l * CONV_LANES, (l + 1) * CONV_LANES)
            left = prev[:, lanes] if s == 0 else um_ref[s * S - H:s * S, lanes].astype(F32)
            right = nxt[:, lanes] if s == CONV_SUB - 1 else um_ref[(s + 1) * S:(s + 1) * S + H, lanes].astype(F32)
            pbuf[l, s * PIN:s * PIN + H, :] = left
            pbuf[l, s * PIN + H:s * PIN + H + S, :] = um_ref[s * S:(s + 1) * S, lanes].astype(F32)
            pbuf[l, s * PIN + H + S:s * PIN + 2 * H + S, :] = right

    off = H - CONV_KERNEL // 2
    for l in range(CONV_SLABS):
        lanes = slice(l * CONV_LANES, (l + 1) * CONV_LANES)
        wv = [jnp.broadcast_to(w_ref[k:k + 1, lanes], (CONV_SUB, CONV_LANES)) for k in range(CONV_KERNEL)]
        bias = jnp.broadcast_to(b_ref[:, lanes], (CONV_SUB, CONV_LANES))

        def conv_body(jb, carry, l=l, wv=wv, bias=bias):
            j0 = jb * CONV_RB
            accs = [bias] * CONV_RB
            for mm in range(CONV_RB + CONV_KERNEL - 1):
                g = pbuf[l, pl.ds(j0 + off + mm, CONV_SUB, stride=PIN), :]
                for jj in range(CONV_RB):
                    k = mm - jj
                    if 0 <= k < CONV_KERNEL:
                        accs[jj] = accs[jj] + g * wv[k]
            for jj in range(CONV_RB):
                ybuf[l, pl.ds((j0 + jj) * CONV_SUB, CONV_SUB), :] = accs[jj]
            return carry

        lax.fori_loop(0, S // CONV_RB, conv_body, 0)

    lgs = [jnp.broadcast_to(lg_ref[:, l * CONV_LANES:(l + 1) * CONV_LANES], (CONV_SUB, CONV_LANES))
           for l in range(CONV_SLABS)]
    lbs = [jnp.broadcast_to(lb_ref[:, l * CONV_LANES:(l + 1) * CONV_LANES], (CONV_SUB, CONV_LANES))
           for l in range(CONV_SLABS)]

    def ln_body(j, carry):
        r0 = pl.multiple_of(j * CONV_SUB, CONV_SUB)
        ys = [ybuf[l, pl.ds(r0, CONV_SUB), :] for l in range(CONV_SLABS)]
        tot = ys[0]
        for y in ys[1:]:
            tot = tot + y
        mu = jnp.sum(tot, axis=-1, keepdims=True) * (1.0 / CONV_WIDTH)
        cs = [y - mu for y in ys]
        sq = cs[0] * cs[0]
        for c in cs[1:]:
            sq = sq + c * c
        var = jnp.sum(sq, axis=-1, keepdims=True) * (1.0 / CONV_WIDTH)
        inv = lax.rsqrt(var + GN_EPS)
        for l in range(CONV_SLABS):
            yn = cs[l] * inv * lgs[l] + lbs[l]
            obuf[l, pl.ds(j, CONV_SUB, stride=POUT), :] = _silu(yn)
        return carry

    lax.fori_loop(0, S, ln_body, 0, unroll=CONV_LN_UNROLL)

    for s in range(CONV_SUB):
        for l in range(CONV_SLABS):
            o_ref[s * S:(s + 1) * S, l * CONV_LANES:(l + 1) * CONV_LANES] = (
                obuf[l, s * POUT:s * POUT + S, :].astype(o_ref.dtype))


def _conv(u, conv_w, conv_b, ln_g, ln_b, batch, seq):
    tm = CONV_TM
    per_seq = seq // tm
    hb = tm // CONV_HALO
    nhb = seq // CONV_HALO

    def prev_map(b, i):
        return (b * nhb + jnp.maximum(i * hb - 1, 0), 0)

    def next_map(b, i):
        return (b * nhb + jnp.minimum((i + 1) * hb, nhb - 1), 0)

    return pl.pallas_call(
        _conv_kernel,
        grid=(batch, per_seq),
        in_specs=[
            pl.BlockSpec((CONV_HALO, CONV_WIDTH), prev_map),
            pl.BlockSpec((tm, CONV_WIDTH), lambda b, i: (b * per_seq + i, 0)),
            pl.BlockSpec((CONV_HALO, CONV_WIDTH), next_map),
            pl.BlockSpec((CONV_KERNEL, CONV_WIDTH), lambda b, i: (0, 0)),
            pl.BlockSpec((1, CONV_WIDTH), lambda b, i: (0, 0)),
            pl.BlockSpec((1, CONV_WIDTH), lambda b, i: (0, 0)),
            pl.BlockSpec((1, CONV_WIDTH), lambda b, i: (0, 0)),
        ],
        out_specs=pl.BlockSpec((tm, CONV_WIDTH), lambda b, i: (b * per_seq + i, 0)),
        out_shape=jax.ShapeDtypeStruct((batch * seq, CONV_WIDTH), BF16),
        scratch_shapes=[
            pltpu.VMEM((CONV_SLABS, CONV_SUB * CONV_PIN, CONV_LANES), F32),
            pltpu.VMEM((CONV_SLABS, CONV_TM, CONV_LANES), F32),
            pltpu.VMEM((CONV_SLABS, CONV_SUB * CONV_POUT, CONV_LANES), F32),
        ],
        compiler_params=_params(("arbitrary", "arbitrary")),
        name="conv",
    )(u, u, u, conv_w, conv_b, ln_g, ln_b)


RS_CS = 4


def _log_gamma(p):
    return jnp.log1p(-jnp.exp2(p))


def _rstate_kernel(kf_ref, vf_ref, kb_ref, vb_ref, pf_ref, pb_ref, rf_ref, rb_ref, sf_ref, sb_ref):
    n = pl.program_id(1)

    @pl.when(n == 0)
    def _init():
        sf_ref[...] = jnp.zeros_like(sf_ref)
        sb_ref[...] = jnp.zeros_like(sb_ref)

    row = lax.broadcasted_iota(jnp.int32, (CHUNK, RET_QK_DIM), 0).astype(F32)
    tdims = (((0,), (0,)), ((), ()))

    def scan(k_ref, v_ref, r_ref, s_ref, p_ref, order, fwd):
        for h in range(RET_HEADS):
            lg = _log_gamma(p_ref[h])
            pos = (CHUNK - 1.0 - row) if fwd else row
            kdec = jnp.exp(pos * lg)
            cdec = jnp.exp(CHUNK * lg)
            state = s_ref[h]
            for c in order:
                rows = slice(c * CHUNK, (c + 1) * CHUNK)
                r_ref[c, h] = state.astype(r_ref.dtype)
                kh = k_ref[rows, h * RET_QK_DIM:(h + 1) * RET_QK_DIM].astype(F32)
                kd = (kh * kdec).astype(BF16)
                vh = v_ref[rows, h * RET_V_DIM:(h + 1) * RET_V_DIM]
                kv = lax.dot_general(kd, vh, tdims, preferred_element_type=F32)
                state = state * cdec + kv
            s_ref[h] = state

    scan(kf_ref, vf_ref, rf_ref, sf_ref, pf_ref, tuple(range(RS_CS)), True)
    scan(kb_ref, vb_ref, rb_ref, sb_ref, pb_ref, tuple(reversed(range(RS_CS))), False)


def _rstate(z, pf, pb, batch, seq):
    tr = RS_CS * CHUNK
    ns = seq // tr
    nchunks = batch * seq // CHUNK
    state = jax.ShapeDtypeStruct((nchunks, RET_HEADS, RET_QK_DIM, RET_V_DIM), BF16)
    fmap = lambda b, n: (b * ns + n)
    bmap = lambda b, n: (b * ns + ns - 1 - n)
    return pl.pallas_call(
        _rstate_kernel,
        grid=(batch, ns),
        in_specs=[
            pl.BlockSpec((tr, RET_QK), lambda b, n: (fmap(b, n), ZB_K)),
            pl.BlockSpec((tr, RET_V), lambda b, n: (fmap(b, n), ZB2_V)),
            pl.BlockSpec((tr, RET_QK), lambda b, n: (bmap(b, n), ZB_K)),
            pl.BlockSpec((tr, RET_V), lambda b, n: (bmap(b, n), ZB2_V)),
            pl.BlockSpec((RET_HEADS, 1, 1), lambda b, n: (0, 0, 0)),
            pl.BlockSpec((RET_HEADS, 1, 1), lambda b, n: (0, 0, 0)),
        ],
        out_specs=[
            pl.BlockSpec((RS_CS, RET_HEADS, RET_QK_DIM, RET_V_DIM), lambda b, n: (fmap(b, n), 0, 0, 0)),
            pl.BlockSpec((RS_CS, RET_HEADS, RET_QK_DIM, RET_V_DIM), lambda b, n: (bmap(b, n), 0, 0, 0)),
        ],
        out_shape=[state, state],
        scratch_shapes=[
            pltpu.VMEM((RET_HEADS, RET_QK_DIM, RET_V_DIM), F32),
            pltpu.VMEM((RET_HEADS, RET_QK_DIM, RET_V_DIM), F32),
        ],
        compiler_params=_params(("arbitrary", "arbitrary")),
        name="rstate",
    )(z, z, z, z, pf, pb)


RO_CS = 4


def _rout_kernel(q_ref, k_ref, v_ref, gr_ref, rf_ref, rb_ref, pf_ref, pb_ref, gn_ref, o_ref):
    ri = lax.broadcasted_iota(jnp.int32, (CHUNK, CHUNK), 0)
    ci = lax.broadcasted_iota(jnp.int32, (CHUNK, CHUNK), 1)
    diff = (ri - ci).astype(F32)
    row = ri.astype(F32)
    ndims = (((1,), (1,)), ((), ()))
    for h in range(RET_HEADS):
        lgf = _log_gamma(pf_ref[h])
        lgb = _log_gamma(pb_ref[h])
        lower = diff >= 0
        upper = diff <= 0
        mask = (jnp.where(lower, jnp.exp(jnp.where(lower, diff, 0.0) * lgf), 0.0)
                + jnp.where(upper, jnp.exp(jnp.where(upper, -diff, 0.0) * lgb), 0.0))
        qdec_f = jnp.exp((row + 1.0) * lgf)
        qdec_b = jnp.exp((CHUNK - row) * lgb)
        gn = gn_ref[:, h * RET_V_DIM:(h + 1) * RET_V_DIM]
        for c in range(RO_CS):
            rows = slice(c * CHUNK, (c + 1) * CHUNK)
            q = q_ref[rows, h * RET_QK_DIM:(h + 1) * RET_QK_DIM]
            k = k_ref[rows, h * RET_QK_DIM:(h + 1) * RET_QK_DIM]
            v = v_ref[rows, h * RET_V_DIM:(h + 1) * RET_V_DIM]
            s = lax.dot_general(q, k, ndims, preferred_element_type=F32)
            inner = _dot((s * mask).astype(BF16), v)
            qf = q.astype(F32)
            qd = jnp.concatenate([(qf * qdec_f).astype(BF16), (qf * qdec_b).astype(BF16)], axis=1)
            rcat = jnp.concatenate([rf_ref[c, h], rb_ref[c, h]], axis=0)
            o = inner + _dot(qd, rcat)
            mu = jnp.mean(o, axis=-1, keepdims=True)
            cen = o - mu
            var = jnp.mean(cen * cen, axis=-1, keepdims=True)
            on = cen * lax.rsqrt(var + GN_EPS) * gn
            gr = gr_ref[rows, h * RET_V_DIM:(h + 1) * RET_V_DIM].astype(F32)
            o_ref[rows, h * RET_V_DIM:(h + 1) * RET_V_DIM] = (on * gr).astype(o_ref.dtype)


def _rout(z, rf, rb, pf, pb, gn_g):
    rows = z.shape[0]
    tr = RO_CS * CHUNK
    state_spec = pl.BlockSpec((RO_CS, RET_HEADS, RET_QK_DIM, RET_V_DIM), lambda i: (i, 0, 0, 0))
    return pl.pallas_call(
        _rout_kernel,
        grid=(rows // tr,),
        in_specs=[
            pl.BlockSpec((tr, RET_QK), lambda i: (i, ZB_Q)),
            pl.BlockSpec((tr, RET_QK), lambda i: (i, ZB_K)),
            pl.BlockSpec((tr, RET_V), lambda i: (i, ZB2_V)),
            pl.BlockSpec((tr, RET_V), lambda i: (i, ZB2_GR)),
            state_spec,
            state_spec,
            pl.BlockSpec((RET_HEADS, 1, 1), lambda i: (0, 0, 0)),
            pl.BlockSpec((RET_HEADS, 1, 1), lambda i: (0, 0, 0)),
            pl.BlockSpec((1, RET_V), lambda i: (0, 0)),
        ],
        out_specs=pl.BlockSpec((tr, RET_V), lambda i: (i, 0)),
        out_shape=jax.ShapeDtypeStruct((rows, RET_V), BF16),
        compiler_params=_params(("arbitrary",)),
        name="rout",
    )(z, z, z, z, rf, rb, pf, pb, gn_g)


MERGE_TM = 512


def _merge_kernel(ca_ref, og_ref, sga_ref, sgb_ref, wc_ref, bc_ref, wr_ref, o_ref):
    for c in range(D_MODEL // MXU_N):
        cols = slice(c * MXU_N, (c + 1) * MXU_N)
        ya = _dot(ca_ref[...], wc_ref[:, cols]) + bc_ref[:, cols]
        yb = _dot(og_ref[...], wr_ref[:, cols])
        m = sga_ref[:, cols].astype(F32) * ya + sgb_ref[:, cols].astype(F32) * yb
        o_ref[:, cols] = m.astype(o_ref.dtype)


def _merge(cact, og, z, wc_b, bc, wr_b):
    rows = cact.shape[0]
    tm = MERGE_TM
    return pl.pallas_call(
        _merge_kernel,
        grid=(rows // tm,),
        in_specs=[
            pl.BlockSpec((tm, CONV_WIDTH), lambda i: (i, 0)),
            pl.BlockSpec((tm, RET_V), lambda i: (i, 0)),
            pl.BlockSpec((tm, D_MODEL), lambda i: (i, ZB2_SGA)),
            pl.BlockSpec((tm, D_MODEL), lambda i: (i, ZB2_SGB)),
            pl.BlockSpec((CONV_WIDTH, D_MODEL), lambda i: (0, 0)),
            pl.BlockSpec((1, D_MODEL), lambda i: (0, 0)),
            pl.BlockSpec((RET_V, D_MODEL), lambda i: (0, 0)),
        ],
        out_specs=pl.BlockSpec((tm, D_MODEL), lambda i: (i, 0)),
        out_shape=jax.ShapeDtypeStruct((rows, D_MODEL), BF16),
        compiler_params=_params(("arbitrary",)),
        name="merge",
    )(cact, og, z, z, wc_b, bc, wr_b)


OUTPROJ_TM = 512


def _outproj_kernel(m_ref, x_ref, wo_ref, gt_ref, g_ref, sc_ref, sh_ref, x1_ref, h2_ref):
    for c in range(D_MODEL // MXU_N):
        cols = slice(c * MXU_N, (c + 1) * MXU_N)
        mix = _dot(m_ref[...], wo_ref[:, cols])
        x1_ref[:, cols] = x_ref[:, cols] + gt_ref[:, cols] * mix
    h2_ref[...] = _modnorm_rows(x1_ref[...], g_ref[...], sc_ref[...], sh_ref[...]).astype(h2_ref.dtype)


def _outproj(merged, x2d, wo_b, mod3, norm2_g, seq):
    rows = merged.shape[0]
    tm = OUTPROJ_TM
    per_seq = seq // tm
    modspec = lambda blk: pl.BlockSpec((None, 1, D_MODEL), lambda i: (i // per_seq, 0, blk))
    return pl.pallas_call(
        _outproj_kernel,
        grid=(rows // tm,),
        in_specs=[
            pl.BlockSpec((tm, D_MODEL), lambda i: (i, 0)),
            pl.BlockSpec((tm, D_MODEL), lambda i: (i, 0)),
            pl.BlockSpec((D_MODEL, D_MODEL), lambda i: (0, 0)),
            modspec(2),
            pl.BlockSpec((1, D_MODEL), lambda i: (0, 0)),
            modspec(4),
            modspec(3),
        ],
        out_specs=[
            pl.BlockSpec((tm, D_MODEL), lambda i: (i, 0)),
            pl.BlockSpec((tm, D_MODEL), lambda i: (i, 0)),
        ],
        out_shape=[
            jax.ShapeDtypeStruct((rows, D_MODEL), F32),
            jax.ShapeDtypeStruct((rows, D_MODEL), BF16),
        ],
        compiler_params=_params(("arbitrary",)),
        name="outproj",
    )(merged, x2d, wo_b, mod3, norm2_g, mod3, mod3)


FFN_TM = 1024
FFN_TF = 512


def _ffn_kernel(h_ref, x1_hbm, wg_ref, wu_ref, wd_ref, gt_ref, fg_ref, o_ref, x1_buf, x1_sem):
    i = pl.program_id(0)
    f = pl.program_id(1)

    def x1_copy():
        return pltpu.make_async_copy(x1_hbm.at[pl.ds(i * FFN_TM, FFN_TM), :], x1_buf, x1_sem)

    @pl.when(f == 0)
    def _start_tile():
        x1_copy().start()
        o_ref[...] = jnp.zeros_like(o_ref)

    acts = []
    for c in range(FFN_TF // MXU_N):
        cols = slice(c * MXU_N, (c + 1) * MXU_N)
        g = _dot(h_ref[...], wg_ref[:, cols])
        u = _dot(h_ref[...], wu_ref[:, cols])
        acts.append((_silu(g) * u).astype(BF16))
    act = jnp.concatenate(acts, axis=1)
    for c in range(D_MODEL // MXU_N):
        cols = slice(c * MXU_N, (c + 1) * MXU_N)
        o_ref[:, cols] += _dot(act, wd_ref[:, cols])

    @pl.when(f == pl.num_programs(1) - 1)
    def _final():
        x1_copy().wait()
        x2 = x1_buf[...] + gt_ref[...] * o_ref[...]
        ms = jnp.mean(x2 * x2, axis=-1, keepdims=True)
        o_ref[...] = x2 * lax.rsqrt(ms + EPS) * fg_ref[...]


def _ffn(h2, x1, w_in_b, w_out_b, mod3, final_g, seq):
    rows = h2.shape[0]
    tm, tf = FFN_TM, FFN_TF
    per_seq = seq // tm
    return pl.pallas_call(
        _ffn_kernel,
        grid=(rows // tm, D_FF // tf),
        in_specs=[
            pl.BlockSpec((tm, D_MODEL), lambda i, f: (i, 0)),
            pl.BlockSpec(memory_space=pl.ANY),
            pl.BlockSpec((D_MODEL, tf), lambda i, f: (0, f)),
            pl.BlockSpec((D_MODEL, tf), lambda i, f: (0, f + D_FF // tf)),
            pl.BlockSpec((tf, D_MODEL), lambda i, f: (f, 0)),
            pl.BlockSpec((None, 1, D_MODEL), lambda i, f: (i // per_seq, 0, 5)),
            pl.BlockSpec((1, D_MODEL), lambda i, f: (0, 0)),
        ],
        out_specs=pl.BlockSpec((tm, D_MODEL), lambda i, f: (i, 0)),
        out_shape=jax.ShapeDtypeStruct((rows, D_MODEL), F32),
        scratch_shapes=[pltpu.VMEM((tm, D_MODEL), F32), pltpu.SemaphoreType.DMA(())],
        compiler_params=_params(("arbitrary", "arbitrary")),
        name="ffn",
    )(h2, x1, w_in_b, w_in_b, w_out_b, mod3, final_g)


def _rope_tables(seq):
    half = RET_QK_DIM // 2
    inv = ROPE_THETA ** (-jnp.arange(half, dtype=F32) / half)
    ang = jnp.arange(seq, dtype=F32)[:, None] * inv[None, :]
    cos = jnp.cos(ang)
    sin = jnp.sin(ang)
    return jnp.concatenate([cos, cos], axis=1), jnp.concatenate([-sin, sin], axis=1)


def _encode(x, mod, w):
    batch, seq, _ = x.shape
    x2d = x.reshape(batch * seq, D_MODEL)
    mod3 = mod.reshape(batch, 1, N_MOD * D_MODEL)
    cos2, sin2 = _rope_tables(seq)
    z, u = _inproj(x2d, w["norm1_g"], mod3, w["w_in"], cos2, sin2, seq)
    cact = _conv(u, w["conv_w"], w["conv_b"], w["ln_g"], w["ln_b"], batch, seq)
    rf, rb = _rstate(z, w["pf"], w["pb"], batch, seq)
    og = _rout(z, rf, rb, w["pf"], w["pb"], w["gn_g"])
    merged = _merge(cact, og, z, w["w_conv_out"], w["b_conv_out"], w["w_ret_out"])
    x1, h2 = _outproj(merged, x2d, w["w_out"], mod3, w["norm2_g"], seq)
    y = _ffn(h2, x1, w["w_ffn_in"], w["w_ffn_out"], mod3, w["final_g"], seq)
    return y.reshape(batch, seq, D_MODEL)


def kernel(x_prompt, x_sample, c_prompt, c_sample, norm1_g, norm2_g, w_ada, b_ada, w_in, conv_dw_w, conv_dw_b, conv_ln_g, conv_ln_b, w_conv_out, b_conv_out, ret_logdecay_fwd, ret_logdecay_bwd, ret_gn_g, w_ret_out, w_out, w_ffn_in, w_ffn_out, final_g):
    assert w_in.shape[0] == 1, "single-layer block"
    nb_p, nb_s = c_prompt.shape[0], c_sample.shape[0]
    pad = (-(nb_p + nb_s)) % 8
    c_all = jnp.concatenate([c_prompt, c_sample, jnp.zeros((pad, D_MODEL), F32)], axis=0)
    mod = _mod(c_all, w_ada[0], b_ada[0].reshape(1, -1))
    w = {
        "norm1_g": norm1_g[0].reshape(1, -1),
        "norm2_g": norm2_g[0].reshape(1, -1),
        "final_g": final_g.reshape(1, -1),
        "w_in": w_in[0].astype(BF16),
        "conv_w": conv_dw_w[0],
        "conv_b": conv_dw_b[0].reshape(1, -1),
        "ln_g": conv_ln_g[0].reshape(1, -1),
        "ln_b": conv_ln_b[0].reshape(1, -1),
        "w_conv_out": w_conv_out[0].astype(BF16),
        "b_conv_out": b_conv_out[0].reshape(1, -1),
        "pf": ret_logdecay_fwd[0].astype(F32).reshape(RET_HEADS, 1, 1),
        "pb": ret_logdecay_bwd[0].astype(F32).reshape(RET_HEADS, 1, 1),
        "gn_g": ret_gn_g[0].reshape(1, -1),
        "w_ret_out": w_ret_out[0].astype(BF16),
        "w_out": w_out[0].astype(BF16),
        "w_ffn_in": w_ffn_in[0].astype(BF16),
        "w_ffn_out": w_ffn_out[0].astype(BF16),
    }
    y_prompt = _encode(x_prompt, mod[:nb_p], w)
    y_sample = _encode(x_sample, mod[nb_p:nb_p + nb_s], w)
    return (y_prompt, y_sample)
```

```python
import jax
import jax.numpy as jnp
from jax import lax
from jax.experimental import pallas as pl
from jax.experimental.pallas import tpu as pltpu

F32 = jnp.float32
BF16 = jnp.bfloat16

D_MODEL = 2048
CONV_WIDTH = D_MODEL // 2
CONV_KERNEL = 31
RET_HEADS = 8
RET_QK_DIM = 128
RET_V_DIM = 256
RET_QK = RET_HEADS * RET_QK_DIM
RET_V = RET_HEADS * RET_V_DIM
CHUNK = 128
D_FF = 5632
ROPE_THETA = 10000.0
EPS = 1e-6
GN_EPS = 1e-5
N_MOD = 6

Z_COLS = 5 * 2048
ZB_Q, ZB_K = 0, 1
ZB2_V, ZB2_GR, ZB2_SGA, ZB2_SGB = 1, 2, 3, 4

MXU_N = 256
VMEM_LIMIT = 60 * 1024 * 1024
CONV_HALO = 16


def _params(sem):
    return pltpu.CompilerParams(dimension_semantics=sem, vmem_limit_bytes=VMEM_LIMIT)


def _sigmoid(x):
    return 0.5 * jnp.tanh(0.5 * x) + 0.5


def _silu(x):
    return x * _sigmoid(x)


def _dot(a, b):
    return jnp.dot(a, b, preferred_element_type=F32)


def _modnorm_rows(x, g, sc, sh):
    ms = jnp.mean(x * x, axis=-1, keepdims=True)
    y = x * lax.rsqrt(ms + EPS) * g
    return y * (1.0 + sc) + sh


def _mod_kernel(c_ref, w_ref, b_ref, o_ref):
    c = c_ref[...]
    s = _silu(c)
    o_ref[...] = _dot(s.astype(BF16), w_ref[...].astype(BF16)) + b_ref[...]


def _mod(c_all, w_ada, b_ada):
    rows = c_all.shape[0]
    tn = 1024
    return pl.pallas_call(
        _mod_kernel,
        grid=(N_MOD * D_MODEL // tn,),
        in_specs=[
            pl.BlockSpec((rows, D_MODEL), lambda j: (0, 0)),
            pl.BlockSpec((D_MODEL, tn), lambda j: (0, j)),
            pl.BlockSpec((1, tn), lambda j: (0, j)),
        ],
        out_specs=pl.BlockSpec((rows, tn), lambda j: (0, j)),
        out_shape=jax.ShapeDtypeStruct((rows, N_MOD * D_MODEL), F32),
        compiler_params=_params(("arbitrary",)),
        name="mod",
    )(c_all, w_ada, b_ada)


INPROJ_TM = 1024
INPROJ_TN = 2048
INPROJ_STEPS = 6
INPROJ_NORM_ROWS = 128


def _inproj_kernel(x_hbm, g_ref, sc_ref, sh_ref, w_ref, cos_ref, sin_ref, z_ref, u_ref, x_buf, h_buf, x_sem):
    i = pl.program_id(0)
    j = pl.program_id(1)
    nchunks = INPROJ_TN // MXU_N

    def x_copy(tile):
        return pltpu.make_async_copy(x_hbm.at[pl.ds(tile * INPROJ_TM, INPROJ_TM), :], x_buf, x_sem)

    @pl.when(j == 0)
    def _modnorm():
        @pl.when(i == 0)
        def _first_fetch():
            x_copy(0).start()

        x_copy(i).wait()
        for r in range(0, INPROJ_TM, INPROJ_NORM_ROWS):
            rows = slice(r, r + INPROJ_NORM_ROWS)
            h_buf[rows, :] = _modnorm_rows(x_buf[rows, :], g_ref[...], sc_ref[...], sh_ref[...]).astype(h_buf.dtype)

        @pl.when(i + 1 < pl.num_programs(0))
        def _next_fetch():
            x_copy(i + 1).start()

    def z_chunk(c):
        return _dot(h_buf[...], w_ref[:, c * MXU_N:(c + 1) * MXU_N])

    def store(c, val):
        z_ref[:, c * MXU_N:(c + 1) * MXU_N] = val.astype(z_ref.dtype)

    @pl.when(j == 0)
    def _glu():
        half = nchunks // 2
        for c in range(half):
            u = z_chunk(c) * _sigmoid(z_chunk(c + half))
            u_ref[:, c * MXU_N:(c + 1) * MXU_N] = u.astype(u_ref.dtype)

    @pl.when(j == 1)
    def _rotary():
        cos2 = cos_ref[...]
        sin2 = sin_ref[...]
        for c in range(nchunks):
            z = z_chunk(c)
            is_k = c * MXU_N >= RET_QK
            outs = []
            for hh in range(MXU_N // RET_QK_DIM):
                zh = z[:, hh * RET_QK_DIM:(hh + 1) * RET_QK_DIM]
                rot = pltpu.roll(zh, RET_QK_DIM // 2, 1)
                r = zh * cos2 + rot * sin2
                outs.append(r * (RET_QK_DIM ** -0.5) if is_k else r)
            store(c, jnp.concatenate(outs, axis=1))

    @pl.when(j == 2)
    def _identity():
        for c in range(nchunks):
            store(c, z_chunk(c))

    @pl.when(j == 3)
    def _swish():
        for c in range(nchunks):
            store(c, _silu(z_chunk(c)))

    @pl.when(j >= 4)
    def _gate():
        for c in range(nchunks):
            store(c, _sigmoid(z_chunk(c)))


def _inproj(x2d, norm_g, mod3, w_in_b, cos2, sin2, seq):
    rows = x2d.shape[0]
    tm, tn = INPROJ_TM, INPROJ_TN
    per_seq = seq // tm
    modspec = lambda blk: pl.BlockSpec((None, 1, D_MODEL), lambda i, j: (i // per_seq, 0, blk))
    return pl.pallas_call(
        _inproj_kernel,
        grid=(rows // tm, INPROJ_STEPS),
        in_specs=[
            pl.BlockSpec(memory_space=pl.ANY),
            pl.BlockSpec((1, D_MODEL), lambda i, j: (0, 0)),
            modspec(1),
            modspec(0),
            pl.BlockSpec((D_MODEL, tn), lambda i, j: (0, j)),
            pl.BlockSpec((tm, RET_QK_DIM), lambda i, j: (i % per_seq, 0)),
            pl.BlockSpec((tm, RET_QK_DIM), lambda i, j: (i % per_seq, 0)),
        ],
        out_specs=[
            pl.BlockSpec((tm, tn), lambda i, j: (i, jnp.maximum(j - 1, 0))),
            pl.BlockSpec((tm, CONV_WIDTH), lambda i, j: (i, 0)),
        ],
        out_shape=[
            jax.ShapeDtypeStruct((rows, Z_COLS), BF16),
            jax.ShapeDtypeStruct((rows, CONV_WIDTH), BF16),
        ],
        scratch_shapes=[
            pltpu.VMEM((tm, D_MODEL), F32),
            pltpu.VMEM((tm, D_MODEL), BF16),
            pltpu.SemaphoreType.DMA(()),
        ],
        compiler_params=_params(("arbitrary", "arbitrary")),
        name="inproj",
    )(x2d, norm_g, mod3, mod3, w_in_b, cos2, sin2)


CONV_TM = 512
CONV_SUB = 8
CONV_LANES = 128
CONV_S = CONV_TM // CONV_SUB
CONV_PIN = CONV_S + 2 * CONV_HALO + 4
CONV_POUT = CONV_S + 4
CONV_RB = 16
CONV_LN_UNROLL = 16
CONV_SLABS = CONV_WIDTH // CONV_LANES
assert (CONV_PIN // 4) % 2 == 1 and (CONV_POUT // 4) % 2 == 1 and CONV_HALO >= CONV_KERNEL // 2


def _conv_kernel(up_ref, um_ref, un_ref, w_ref, b_ref, lg_ref, lb_ref, o_ref, pbuf, ybuf, obuf):
    i = pl.program_id(1)
    last = pl.num_programs(1) - 1
    S, H, PIN, POUT = CONV_S, CONV_HALO, CONV_PIN, CONV_POUT
    prev = jnp.where(i > 0, up_ref[...].astype(F32), 0.0)
    nxt = jnp.where(i < last, un_ref[...].astype(F32), 0.0)

    for s in range(CONV_SUB):
        for l in range(CONV_SLABS):
            lanes = slice(l * CONV_LANES, (l + 1) * CONV_LANES)
            left = prev[:, lanes] if s == 0 else um_ref[s * S - H:s * S, lanes].astype(F32)
            right = nxt[:, lanes] if s == CONV_SUB - 1 else um_ref[(s + 1) * S:(s + 1) * S + H, lanes].astype(F32)
            pbuf[l, s * PIN:s * PIN + H, :] = left
            pbuf[l, s * PIN + H:s * PIN + H + S, :] = um_ref[s * S:(s + 1) * S, lanes].astype(F32)
            pbuf[l, s * PIN + H + S:s * PIN + 2 * H + S, :] = right

    off = H - CONV_KERNEL // 2
    for l in range(CONV_SLABS):
        lanes = slice(l * CONV_LANES, (l + 1) * CONV_LANES)
        wv = [jnp.broadcast_to(w_ref[k:k + 1, lanes], (CONV_SUB, CONV_LANES)) for k in range(CONV_KERNEL)]
        bias = jnp.broadcast_to(b_ref[:, lanes], (CONV_SUB, CONV_LANES))

        def conv_body(jb, carry, l=l, wv=wv, bias=bias):
            j0 = jb * CONV_RB
            accs = [bias] * CONV_RB
            for mm in range(CONV_RB + CONV_KERNEL - 1):
                g = pbuf[l, pl.ds(j0 + off + mm, CONV_SUB, stride=PIN), :]
                for jj in range(CONV_RB):
                    k = mm - jj
                    if 0 <= k < CONV_KERNEL:
                        accs[jj] = accs[jj] + g * wv[k]
            for jj in range(CONV_RB):
                ybuf[l, pl.ds((j0 + jj) * CONV_SUB, CONV_SUB), :] = accs[jj]
            return carry

        lax.fori_loop(0, S // CONV_RB, conv_body, 0)

    lgs = [jnp.broadcast_to(lg_ref[:, l * CONV_LANES:(l + 1) * CONV_LANES], (CONV_SUB, CONV_LANES))
           for l in range(CONV_SLABS)]
    lbs = [jnp.broadcast_to(lb_ref[:, l * CONV_LANES:(l + 1) * CONV_LANES], (CONV_SUB, CONV_LANES))
           for l in range(CONV_SLABS)]

    def ln_body(j, carry):
        r0 = pl.multiple_of(j * CONV_SUB, CONV_SUB)
        ys = [ybuf[l, pl.ds(r0, CONV_SUB), :] for l in range(CONV_SLABS)]
        tot = ys[0]
        for y in ys[1:]:
            tot = tot + y
        mu = jnp.sum(tot, axis=-1, keepdims=True) * (1.0 / CONV_WIDTH)
        cs = [y - mu for y in ys]
        sq = cs[0] * cs[0]
        for c in cs[1:]:
            sq = sq + c * c
        var = jnp.sum(sq, axis=-1, keepdims=True) * (1.0 / CONV_WIDTH)
        inv = lax.rsqrt(var + GN_EPS)
        for l in range(CONV_SLABS):
            yn = cs[l] * inv * lgs[l] + lbs[l]
            obuf[l, pl.ds(j, CONV_SUB, stride=POUT), :] = _silu(yn)
        return carry

    lax.fori_loop(0, S, ln_body, 0, unroll=CONV_LN_UNROLL)

    for s in range(CONV_SUB):
        for l in range(CONV_SLABS):
            o_ref[s * S:(s + 1) * S, l * CONV_LANES:(l + 1) * CONV_LANES] = (
                obuf[l, s * POUT:s * POUT + S, :].astype(o_ref.dtype))


def _conv(u, conv_w, conv_b, ln_g, ln_b, batch, seq):
    tm = CONV_TM
    per_seq = seq // tm
    hb = tm // CONV_HALO
    nhb = seq // CONV_HALO

    def prev_map(b, i):
        return (b * nhb + jnp.maximum(i * hb - 1, 0), 0)

    def next_map(b, i):
        return (b * nhb + jnp.minimum((i + 1) * hb, nhb - 1), 0)

    return pl.pallas_call(
        _conv_kernel,
        grid=(batch, per_seq),
        in_specs=[
            pl.BlockSpec((CONV_HALO, CONV_WIDTH), prev_map),
            pl.BlockSpec((tm, CONV_WIDTH), lambda b, i: (b * per_seq + i, 0)),
            pl.BlockSpec((CONV_HALO, CONV_WIDTH), next_map),
            pl.BlockSpec((CONV_KERNEL, CONV_WIDTH), lambda b, i: (0, 0)),
            pl.BlockSpec((1, CONV_WIDTH), lambda b, i: (0, 0)),
            pl.BlockSpec((1, CONV_WIDTH), lambda b, i: (0, 0)),
            pl.BlockSpec((1, CONV_WIDTH), lambda b, i: (0, 0)),
        ],
        out_specs=pl.BlockSpec((tm, CONV_WIDTH), lambda b, i: (b * per_seq + i, 0)),
        out_shape=jax.ShapeDtypeStruct((batch * seq, CONV_WIDTH), BF16),
        scratch_shapes=[
            pltpu.VMEM((CONV_SLABS, CONV_SUB * CONV_PIN, CONV_LANES), F32),
            pltpu.VMEM((CONV_SLABS, CONV_TM, CONV_LANES), F32),
            pltpu.VMEM((CONV_SLABS, CONV_SUB * CONV_POUT, CONV_LANES), F32),
        ],
        compiler_params=_params(("arbitrary", "arbitrary")),
        name="conv",
    )(u, u, u, conv_w, conv_b, ln_g, ln_b)


RS_CS = 4


def _log_gamma(p):
    return jnp.log1p(-jnp.exp2(p))


def _rstate_kernel(kf_ref, vf_ref, kb_ref, vb_ref, pf_ref, pb_ref, rf_ref, rb_ref, sf_ref, sb_ref):
    n = pl.program_id(1)

    @pl.when(n == 0)
    def _init():
        sf_ref[...] = jnp.zeros_like(sf_ref)
        sb_ref[...] = jnp.zeros_like(sb_ref)

    row = lax.broadcasted_iota(jnp.int32, (CHUNK, RET_QK_DIM), 0).astype(F32)
    tdims = (((0,), (0,)), ((), ()))

    def scan(k_ref, v_ref, r_ref, s_ref, p_ref, order, fwd):
        for h in range(RET_HEADS):
            lg = _log_gamma(p_ref[h])
            pos = (CHUNK - 1.0 - row) if fwd else row
            kdec = jnp.exp(pos * lg)
            cdec = jnp.exp(CHUNK * lg)
            state = s_ref[h]
            for c in order:
                rows = slice(c * CHUNK, (c + 1) * CHUNK)
                r_ref[c, h] = state.astype(r_ref.dtype)
                kh = k_ref[rows, h * RET_QK_DIM:(h + 1) * RET_QK_DIM].astype(F32)
                kd = (kh * kdec).astype(BF16)
                vh = v_ref[rows, h * RET_V_DIM:(h + 1) * RET_V_DIM]
                kv = lax.dot_general(kd, vh, tdims, preferred_element_type=F32)
                state = state * cdec + kv
            s_ref[h] = state

    scan(kf_ref, vf_ref, rf_ref, sf_ref, pf_ref, tuple(range(RS_CS)), True)
    scan(kb_ref, vb_ref, rb_ref, sb_ref, pb_ref, tuple(reversed(range(RS_CS))), False)


def _rstate(z, pf, pb, batch, seq):
    tr = RS_CS * CHUNK
    ns = seq // tr
    nchunks = batch * seq // CHUNK
    state = jax.ShapeDtypeStruct((nchunks, RET_HEADS, RET_QK_DIM, RET_V_DIM), BF16)
    fmap = lambda b, n: (b * ns + n)
    bmap = lambda b, n: (b * ns + ns - 1 - n)
    return pl.pallas_call(
        _rstate_kernel,
        grid=(batch, ns),
        in_specs=[
            pl.BlockSpec((tr, RET_QK), lambda b, n: (fmap(b, n), ZB_K)),
            pl.BlockSpec((tr, RET_V), lambda b, n: (fmap(b, n), ZB2_V)),
            pl.BlockSpec((tr, RET_QK), lambda b, n: (bmap(b, n), ZB_K)),
            pl.BlockSpec((tr, RET_V), lambda b, n: (bmap(b, n), ZB2_V)),
            pl.BlockSpec((RET_HEADS, 1, 1), lambda b, n: (0, 0, 0)),
            pl.BlockSpec((RET_HEADS, 1, 1), lambda b, n: (0, 0, 0)),
        ],
        out_specs=[
            pl.BlockSpec((RS_CS, RET_HEADS, RET_QK_DIM, RET_V_DIM), lambda b, n: (fmap(b, n), 0, 0, 0)),
            pl.BlockSpec((RS_CS, RET_HEADS, RET_QK_DIM, RET_V_DIM), lambda b, n: (bmap(b, n), 0, 0, 0)),
        ],
        out_shape=[state, state],
        scratch_shapes=[
            pltpu.VMEM((RET_HEADS, RET_QK_DIM, RET_V_DIM), F32),
            pltpu.VMEM((RET_HEADS, RET_QK_DIM, RET_V_DIM), F32),
        ],
        compiler_params=_params(("arbitrary", "arbitrary")),
        name="rstate",
    )(z, z, z, z, pf, pb)


RO_CS = 4


def _rout_kernel(q_ref, k_ref, v_ref, gr_ref, rf_ref, rb_ref, pf_ref, pb_ref, gn_ref, o_ref):
    ri = lax.broadcasted_iota(jnp.int32, (CHUNK, CHUNK), 0)
    ci = lax.broadcasted_iota(jnp.int32, (CHUNK, CHUNK), 1)
    diff = (ri - ci).astype(F32)
    row = ri.astype(F32)
    ndims = (((1,), (1,)), ((), ()))
    for h in range(RET_HEADS):
        lgf = _log_gamma(pf_ref[h])
        lgb = _log_gamma(pb_ref[h])
        lower = diff >= 0
        upper = diff <= 0
        mask = (jnp.where(lower, jnp.exp(jnp.where(lower, diff, 0.0) * lgf), 0.0)
                + jnp.where(upper, jnp.exp(jnp.where(upper, -diff, 0.0) * lgb), 0.0))
        qdec_f = jnp.exp((row + 1.0) * lgf)
        qdec_b = jnp.exp((CHUNK - row) * lgb)
        gn = gn_ref[:, h * RET_V_DIM:(h + 1) * RET_V_DIM]
        for c in range(RO_CS):
            rows = slice(c * CHUNK, (c + 1) * CHUNK)
            q = q_ref[rows, h * RET_QK_DIM:(h + 1) * RET_QK_DIM]
            k = k_ref[rows, h * RET_QK_DIM:(h + 1) * RET_QK_DIM]
            v = v_ref[rows, h * RET_V_DIM:(h + 1) * RET_V_DIM]
            s = lax.dot_general(q, k, ndims, preferred_element_type=F32)
            inner = _dot((s * mask).astype(BF16), v)
            qf = q.astype(F32)
            qd = jnp.concatenate([(qf * qdec_f).astype(BF16), (qf * qdec_b).astype(BF16)], axis=1)
            rcat = jnp.concatenate([rf_ref[c, h], rb_ref[c, h]], axis=0)
            o = inner + _dot(qd, rcat)
            mu = jnp.mean(o, axis=-1, keepdims=True)
            cen = o - mu
            var = jnp.mean(cen * cen, axis=-1, keepdims=True)
            on = cen * lax.rsqrt(var + GN_EPS) * gn
            gr = gr_ref[rows, h * RET_V_DIM:(h + 1) * RET_V_DIM].astype(F32)
            o_ref[rows, h * RET_V_DIM:(h + 1) * RET_V_DIM] = (on * gr).astype(o_ref.dtype)


def _rout(z, rf, rb, pf, pb, gn_g):
    rows = z.shape[0]
    tr = RO_CS * CHUNK
    state_spec = pl.BlockSpec((RO_CS, RET_HEADS, RET_QK_DIM, RET_V_DIM), lambda i: (i, 0, 0, 0))
    return pl.pallas_call(
        _rout_kernel,
        grid=(rows // tr,),
        in_specs=[
            pl.BlockSpec((tr, RET_QK), lambda i: (i, ZB_Q)),
            pl.BlockSpec((tr, RET_QK), lambda i: (i, ZB_K)),
            pl.BlockSpec((tr, RET_V), lambda i: (i, ZB2_V)),
            pl.BlockSpec((tr, RET_V), lambda i: (i, ZB2_GR)),
            state_spec,
            state_spec,
            pl.BlockSpec((RET_HEADS, 1, 1), lambda i: (0, 0, 0)),
            pl.BlockSpec((RET_HEADS, 1, 1), lambda i: (0, 0, 0)),
            pl.BlockSpec((1, RET_V), lambda i: (0, 0)),
        ],
        out_specs=pl.BlockSpec((tr, RET_V), lambda i: (i, 0)),
        out_shape=jax.ShapeDtypeStruct((rows, RET_V), BF16),
        compiler_params=_params(("arbitrary",)),
        name="rout",
    )(z, z, z, z, rf, rb, pf, pb, gn_g)


MERGE_TM = 512


def _merge_kernel(ca_ref, og_ref, sga_ref, sgb_ref, wc_ref, bc_ref, wr_ref, o_ref):
    for c in range(D_MODEL // MXU_N):
        cols = slice(c * MXU_N, (c + 1) * MXU_N)
        ya = _dot(ca_ref[...], wc_ref[:, cols]) + bc_ref[:, cols]
        yb = _dot(og_ref[...], wr_ref[:, cols])
        m = sga_ref[:, cols].astype(F32) * ya + sgb_ref[:, cols].astype(F32) * yb
        o_ref[:, cols] = m.astype(o_ref.dtype)


def _merge(cact, og, z, wc_b, bc, wr_b):
    rows = cact.shape[0]
    tm = MERGE_TM
    return pl.pallas_call(
        _merge_kernel,
        grid=(rows // tm,),
        in_specs=[
            pl.BlockSpec((tm, CONV_WIDTH), lambda i: (i, 0)),
            pl.BlockSpec((tm, RET_V), lambda i: (i, 0)),
            pl.BlockSpec((tm, D_MODEL), lambda i: (i, ZB2_SGA)),
            pl.BlockSpec((tm, D_MODEL), lambda i: (i, ZB2_SGB)),
            pl.BlockSpec((CONV_WIDTH, D_MODEL), lambda i: (0, 0)),
            pl.BlockSpec((1, D_MODEL), lambda i: (0, 0)),
            pl.BlockSpec((RET_V, D_MODEL), lambda i: (0, 0)),
        ],
        out_specs=pl.BlockSpec((tm, D_MODEL), lambda i: (i, 0)),
        out_shape=jax.ShapeDtypeStruct((rows, D_MODEL), BF16),
        compiler_params=_params(("arbitrary",)),
        name="merge",
    )(cact, og, z, z, wc_b, bc, wr_b)


OUTPROJ_TM = 512


def _outproj_kernel(m_ref, x_ref, wo_ref, gt_ref, g_ref, sc_ref, sh_ref, x1_ref, h2_ref):
    for c in range(D_MODEL // MXU_N):
        cols = slice(c * MXU_N, (c + 1) * MXU_N)
        mix = _dot(m_ref[...], wo_ref[:, cols])
        x1_ref[:, cols] = x_ref[:, cols] + gt_ref[:, cols] * mix
    h2_ref[...] = _modnorm_rows(x1_ref[...], g_ref[...], sc_ref[...], sh_ref[...]).astype(h2_ref.dtype)


def _outproj(merged, x2d, wo_b, mod3, norm2_g, seq):
    rows = merged.shape[0]
    tm = OUTPROJ_TM
    per_seq = seq // tm
    modspec = lambda blk: pl.BlockSpec((None, 1, D_MODEL), lambda i: (i // per_seq, 0, blk))
    return pl.pallas_call(
        _outproj_kernel,
        grid=(rows // tm,),
        in_specs=[
            pl.BlockSpec((tm, D_MODEL), lambda i: (i, 0)),
            pl.BlockSpec((tm, D_MODEL), lambda i: (i, 0)),
            pl.BlockSpec((D_MODEL, D_MODEL), lambda i: (0, 0)),
            modspec(2),
            pl.BlockSpec((1, D_MODEL), lambda i: (0, 0)),
            modspec(4),
            modspec(3),
        ],
        out_specs=[
            pl.BlockSpec((tm, D_MODEL), lambda i: (i, 0)),
            pl.BlockSpec((tm, D_MODEL), lambda i: (i, 0)),
        ],
        out_shape=[
            jax.ShapeDtypeStruct((rows, D_MODEL), F32),
            jax.ShapeDtypeStruct((rows, D_MODEL), BF16),
        ],
        compiler_params=_params(("arbitrary",)),
        name="outproj",
    )(merged, x2d, wo_b, mod3, norm2_g, mod3, mod3)


FFN_TM = 1024
FFN_TF = 512


def _ffn_kernel(h_ref, x1_hbm, wg_ref, wu_ref, wd_ref, gt_ref, fg_ref, o_ref, x1_buf, x1_sem):
    i = pl.program_id(0)
    f = pl.program_id(1)

    def x1_copy():
        return pltpu.make_async_copy(x1_hbm.at[pl.ds(i * FFN_TM, FFN_TM), :], x1_buf, x1_sem)

    @pl.when(f == 0)
    def _start_tile():
        x1_copy().start()
        o_ref[...] = jnp.zeros_like(o_ref)

    acts = []
    for c in range(FFN_TF // MXU_N):
        cols = slice(c * MXU_N, (c + 1) * MXU_N)
        g = _dot(h_ref[...], wg_ref[:, cols])
        u = _dot(h_ref[...], wu_ref[:, cols])
        acts.append((_silu(g) * u).astype(BF16))
    act = jnp.concatenate(acts, axis=1)
    for c in range(D_MODEL // MXU_N):
        cols = slice(c * MXU_N, (c + 1) * MXU_N)
        o_ref[:, cols] += _dot(act, wd_ref[:, cols])

    @pl.when(f == pl.num_programs(1) - 1)
    def _final():
        x1_copy().wait()
        x2 = x1_buf[...] + gt_ref[...] * o_ref[...]
        ms = jnp.mean(x2 * x2, axis=-1, keepdims=True)
        o_ref[...] = x2 * lax.rsqrt(ms + EPS) * fg_ref[...]


def _ffn(h2, x1, w_in_b, w_out_b, mod3, final_g, seq):
    rows = h2.shape[0]
    tm, tf = FFN_TM, FFN_TF
    per_seq = seq // tm
    return pl.pallas_call(
        _ffn_kernel,
        grid=(rows // tm, D_FF // tf),
        in_specs=[
            pl.BlockSpec((tm, D_MODEL), lambda i, f: (i, 0)),
            pl.BlockSpec(memory_space=pl.ANY),
            pl.BlockSpec((D_MODEL, tf), lambda i, f: (0, f)),
            pl.BlockSpec((D_MODEL, tf), lambda i, f: (0, f + D_FF // tf)),
            pl.BlockSpec((tf, D_MODEL), lambda i, f: (f, 0)),
            pl.BlockSpec((None, 1, D_MODEL), lambda i, f: (i // per_seq, 0, 5)),
            pl.BlockSpec((1, D_MODEL), lambda i, f: (0, 0)),
        ],
        out_specs=pl.BlockSpec((tm, D_MODEL), lambda i, f: (i, 0)),
        out_shape=jax.ShapeDtypeStruct((rows, D_MODEL), F32),
        scratch_shapes=[pltpu.VMEM((tm, D_MODEL), F32), pltpu.SemaphoreType.DMA(())],
        compiler_params=_params(("arbitrary", "arbitrary")),
        name="ffn",
    )(h2, x1, w_in_b, w_in_b, w_out_b, mod3, final_g)


def _rope_tables(seq):
    half = RET_QK_DIM // 2
    inv = ROPE_THETA ** (-jnp.arange(half, dtype=F32) / half)
    ang = jnp.arange(seq, dtype=F32)[:, None] * inv[None, :]
    cos = jnp.cos(ang)
    sin = jnp.sin(ang)
    return jnp.concatenate([cos, cos], axis=1), jnp.concatenate([-sin, sin], axis=1)


def _encode(x, mod, w):
    batch, seq, _ = x.shape
    x2d = x.reshape(batch * seq, D_MODEL)
    mod3 = mod.reshape(batch, 1, N_MOD * D_MODEL)
    cos2, sin2 = _rope_tables(seq)
    z, u = _inproj(x2d, w["norm1_g"], mod3, w["w_in"], cos2, sin2, seq)
    cact = _conv(u, w["conv_w"], w["conv_b"], w["ln_g"], w["ln_b"], batch, seq)
    rf, rb = _rstate(z, w["pf"], w["pb"], batch, seq)
    og = _rout(z, rf, rb, w["pf"], w["pb"], w["gn_g"])
    merged = _merge(cact, og, z, w["w_conv_out"], w["b_conv_out"], w["w_ret_out"])
    x1, h2 = _outproj(merged, x2d, w["w_out"], mod3, w["norm2_g"], seq)
    y = _ffn(h2, x1, w["w_ffn_in"], w["w_ffn_out"], mod3, w["final_g"], seq)
    return y.reshape(batch, seq, D_MODEL)


def kernel(x_prompt, x_sample, c_prompt, c_sample, norm1_g, norm2_g, w_ada, b_ada, w_in, conv_dw_w, conv_dw_b, conv_ln_g, conv_ln_b, w_conv_out, b_conv_out, ret_logdecay_fwd, ret_logdecay_bwd, ret_gn_g, w_ret_out, w_out, w_ffn_in, w_ffn_out, final_g):
    assert w_in.shape[0] == 1, "single-layer block"
    nb_p, nb_s = c_prompt.shape[0], c_sample.shape[0]
    pad = (-(nb_p + nb_s)) % 8
    c_all = jnp.concatenate([c_prompt, c_sample, jnp.zeros((pad, D_MODEL), F32)], axis=0)
    mod = _mod(c_all, w_ada[0], b_ada[0].reshape(1, -1))
    w = {
        "norm1_g": norm1_g[0].reshape(1, -1),
        "norm2_g": norm2_g[0].reshape(1, -1),
        "final_g": final_g.reshape(1, -1),
        "w_in": w_in[0].astype(BF16),
        "conv_w": conv_dw_w[0],
        "conv_b": conv_dw_b[0].reshape(1, -1),
        "ln_g": conv_ln_g[0].reshape(1, -1),
        "ln_b": conv_ln_b[0].reshape(1, -1),
        "w_conv_out": w_conv_out[0].astype(BF16),
        "b_conv_out": b_conv_out[0].reshape(1, -1),
        "pf": ret_logdecay_fwd[0].astype(F32).reshape(RET_HEADS, 1, 1),
        "pb": ret_logdecay_bwd[0].astype(F32).reshape(RET_HEADS, 1, 1),
        "gn_g": ret_gn_g[0].reshape(1, -1),
        "w_ret_out": w_ret_out[0].astype(BF16),
        "w_out": w_out[0].astype(BF16),
        "w_ffn_in": w_ffn_in[0].astype(BF16),
        "w_ffn_out": w_ffn_out[0].astype(BF16),
    }
    y_prompt = _encode(x_prompt, mod[:nb_p], w)
    y_sample = _encode(x_sample, mod[nb_p:nb_p + nb_s], w)
    return (y_prompt, y_sample)
```

```python
import jax
import jax.numpy as jnp
from jax import lax
from jax.experimental import pallas as pl
from jax.experimental.pallas import tpu as pltpu

F32 = jnp.float32
BF16 = jnp.bfloat16

D_MODEL = 2048
CONV_WIDTH = D_MODEL // 2
CONV_KERNEL = 31
RET_HEADS = 8
RET_QK_DIM = 128
RET_V_DIM = 256
RET_QK = RET_HEADS * RET_QK_DIM
RET_V = RET_HEADS * RET_V_DIM
CHUNK = 128
D_FF = 5632
ROPE_THETA = 10000.0
EPS = 1e-6
GN_EPS = 1e-5
N_MOD = 6

Z_COLS = 5 * 2048
ZB_Q, ZB_K = 0, 1
ZB2_V, ZB2_GR, ZB2_SGA, ZB2_SGB = 1, 2, 3, 4

MXU_N = 256
VMEM_LIMIT = 60 * 1024 * 1024
CONV_HALO = 16


def _params(sem):
    return pltpu.CompilerParams(dimension_semantics=sem, vmem_limit_bytes=VMEM_LIMIT)


def _sigmoid(x):
    return 0.5 * jnp.tanh(0.5 * x) + 0.5


def _silu(x):
    return x * _sigmoid(x)


def _dot(a, b):
    return jnp.dot(a, b, preferred_element_type=F32)


def _modnorm_rows(x, g, sc, sh):
    ms = jnp.mean(x * x, axis=-1, keepdims=True)
    y = x * lax.rsqrt(ms + EPS) * g
    return y * (1.0 + sc) + sh


def _mod_kernel(c_ref, w_ref, b_ref, o_ref):
    c = c_ref[...]
    s = _silu(c)
    o_ref[...] = _dot(s.astype(BF16), w_ref[...].astype(BF16)) + b_ref[...]


def _mod(c_all, w_ada, b_ada):
    rows = c_all.shape[0]
    tn = 1024
    return pl.pallas_call(
        _mod_kernel,
        grid=(N_MOD * D_MODEL // tn,),
        in_specs=[
            pl.BlockSpec((rows, D_MODEL), lambda j: (0, 0)),
            pl.BlockSpec((D_MODEL, tn), lambda j: (0, j)),
            pl.BlockSpec((1, tn), lambda j: (0, j)),
        ],
        out_specs=pl.BlockSpec((rows, tn), lambda j: (0, j)),
        out_shape=jax.ShapeDtypeStruct((rows, N_MOD * D_MODEL), F32),
        compiler_params=_params(("arbitrary",)),
        name="mod",
    )(c_all, w_ada, b_ada)


INPROJ_TM = 1024
INPROJ_TN = 2048
INPROJ_STEPS = 6
INPROJ_NORM_ROWS = 128


def _inproj_kernel(x_hbm, g_ref, sc_ref, sh_ref, w_ref, cos_ref, sin_ref, z_ref, u_ref, x_buf, h_buf, x_sem):
    i = pl.program_id(0)
    j = pl.program_id(1)
    nchunks = INPROJ_TN // MXU_N

    def x_copy(tile):
        return pltpu.make_async_copy(x_hbm.at[pl.ds(tile * INPROJ_TM, INPROJ_TM), :], x_buf, x_sem)

    @pl.when(j == 0)
    def _modnorm():
        @pl.when(i == 0)
        def _first_fetch():
            x_copy(0).start()

        x_copy(i).wait()
        for r in range(0, INPROJ_TM, INPROJ_NORM_ROWS):
            rows = slice(r, r + INPROJ_NORM_ROWS)
            h_buf[rows, :] = _modnorm_rows(x_buf[rows, :], g_ref[...], sc_ref[...], sh_ref[...]).astype(h_buf.dtype)

        @pl.when(i + 1 < pl.num_programs(0))
        def _next_fetch():
            x_copy(i + 1).start()

    def z_chunk(c):
        return _dot(h_buf[...], w_ref[:, c * MXU_N:(c + 1) * MXU_N])

    def store(c, val):
        z_ref[:, c * MXU_N:(c + 1) * MXU_N] = val.astype(z_ref.dtype)

    @pl.when(j == 0)
    def _glu():
        half = nchunks // 2
        for c in range(half):
            u = z_chunk(c) * _sigmoid(z_chunk(c + half))
            u_ref[:, c * MXU_N:(c + 1) * MXU_N] = u.astype(u_ref.dtype)

    @pl.when(j == 1)
    def _rotary():
        cos2 = cos_ref[...]
        sin2 = sin_ref[...]
        for c in range(nchunks):
            z = z_chunk(c)
            is_k = c * MXU_N >= RET_QK
            outs = []
            for hh in range(MXU_N // RET_QK_DIM):
                zh = z[:, hh * RET_QK_DIM:(hh + 1) * RET_QK_DIM]
                rot = pltpu.roll(zh, RET_QK_DIM // 2, 1)
                r = zh * cos2 + rot * sin2
                outs.append(r * (RET_QK_DIM ** -0.5) if is_k else r)
            store(c, jnp.concatenate(outs, axis=1))

    @pl.when(j == 2)
    def _identity():
        for c in range(nchunks):
            store(c, z_chunk(c))

    @pl.when(j == 3)
    def _swish():
        for c in range(nchunks):
            store(c, _silu(z_chunk(c)))

    @pl.when(j >= 4)
    def _gate():
        for c in range(nchunks):
            store(c, _sigmoid(z_chunk(c)))


def _inproj(x2d, norm_g, mod3, w_in_b, cos2, sin2, seq):
    rows = x2d.shape[0]
    tm, tn = INPROJ_TM, INPROJ_TN
    per_seq = seq // tm
    modspec = lambda blk: pl.BlockSpec((None, 1, D_MODEL), lambda i, j: (i // per_seq, 0, blk))
    return pl.pallas_call(
        _inproj_kernel,
        grid=(rows // tm, INPROJ_STEPS),
        in_specs=[
            pl.BlockSpec(memory_space=pl.ANY),
            pl.BlockSpec((1, D_MODEL), lambda i, j: (0, 0)),
            modspec(1),
            modspec(0),
            pl.BlockSpec((D_MODEL, tn), lambda i, j: (0, j)),
            pl.BlockSpec((tm, RET_QK_DIM), lambda i, j: (i % per_seq, 0)),
            pl.BlockSpec((tm, RET_QK_DIM), lambda i, j: (i % per_seq, 0)),
        ],
        out_specs=[
            pl.BlockSpec((tm, tn), lambda i, j: (i, jnp.maximum(j - 1, 0))),
            pl.BlockSpec((tm, CONV_WIDTH), lambda i, j: (i, 0)),
        ],
        out_shape=[
            jax.ShapeDtypeStruct((rows, Z_COLS), BF16),
            jax.ShapeDtypeStruct((rows, CONV_WIDTH), BF16),
        ],
        scratch_shapes=[
            pltpu.VMEM((tm, D_MODEL), F32),
            pltpu.VMEM((tm, D_MODEL), BF16),
            pltpu.SemaphoreType.DMA(()),
        ],
        compiler_params=_params(("arbitrary", "arbitrary")),
        name="inproj",
    )(x2d, norm_g, mod3, mod3, w_in_b, cos2, sin2)


CONV_TM = 1024
CONV_SUB = 8
CONV_LANES = 128
CONV_S = CONV_TM // CONV_SUB
CONV_PIN = CONV_S + 2 * CONV_HALO + 4
CONV_POUT = CONV_S + 4
CONV_RB = 16
CONV_LN_UNROLL = 16
CONV_SLABS = CONV_WIDTH // CONV_LANES
assert (CONV_PIN // 4) % 2 == 1 and (CONV_POUT // 4) % 2 == 1 and CONV_HALO >= CONV_KERNEL // 2


def _conv_kernel(up_ref, um_ref, un_ref, w_ref, b_ref, lg_ref, lb_ref, o_ref, pbuf, ybuf, obuf):
    i = pl.program_id(1)
    last = pl.num_programs(1) - 1
    S, H, PIN, POUT = CONV_S, CONV_HALO, CONV_PIN, CONV_POUT
    prev = jnp.where(i > 0, up_ref[...].astype(F32), 0.0)
    nxt = jnp.where(i < last, un_ref[...].astype(F32), 0.0)

    for s in range(CONV_SUB):
        for l in range(CONV_SLABS):
            lanes = slice(l * CONV_LANES, (l + 1) * CONV_LANES)
            left = prev[:, lanes] if s == 0 else um_ref[s * S - H:s * S, lanes].astype(F32)
            right = nxt[:, lanes] if s == CONV_SUB - 1 else um_ref[(s + 1) * S:(s + 1) * S + H, lanes].astype(F32)
            pbuf[l, s * PIN:s * PIN + H, :] = left
            pbuf[l, s * PIN + H:s * PIN + H + S, :] = um_ref[s * S:(s + 1) * S, lanes].astype(F32)
            pbuf[l, s * PIN + H + S:s * PIN + 2 * H + S, :] = right

    off = H - CONV_KERNEL // 2
    for l in range(CONV_SLABS):
        lanes = slice(l * CONV_LANES, (l + 1) * CONV_LANES)
        wv = [jnp.broadcast_to(w_ref[k:k + 1, lanes], (CONV_SUB, CONV_LANES)) for k in range(CONV_KERNEL)]
        bias = jnp.broadcast_to(b_ref[:, lanes], (CONV_SUB, CONV_LANES))

        def conv_body(jb, carry, l=l, wv=wv, bias=bias):
            j0 = jb * CONV_RB
            accs = [bias] * CONV_RB
            for mm in range(CONV_RB + CONV_KERNEL - 1):
                g = pbuf[l, pl.ds(j0 + off + mm, CONV_SUB, stride=PIN), :]
                for jj in range(CONV_RB):
                    k = mm - jj
                    if 0 <= k < CONV_KERNEL:
                        accs[jj] = accs[jj] + g * wv[k]
            for jj in range(CONV_RB):
                ybuf[l, pl.ds((j0 + jj) * CONV_SUB, CONV_SUB), :] = accs[jj]
            return carry

        lax.fori_loop(0, S // CONV_RB, conv_body, 0)

    lgs = [jnp.broadcast_to(lg_ref[:, l * CONV_LANES:(l + 1) * CONV_LANES], (CONV_SUB, CONV_LANES))
           for l in range(CONV_SLABS)]
    lbs = [jnp.broadcast_to(lb_ref[:, l * CONV_LANES:(l + 1) * CONV_LANES], (CONV_SUB, CONV_LANES))
           for l in range(CONV_SLABS)]

    def ln_body(j, carry):
        r0 = pl.multiple_of(j * CONV_SUB, CONV_SUB)
        ys = [ybuf[l, pl.ds(r0, CONV_SUB), :] for l in range(CONV_SLABS)]
        tot = ys[0]
        for y in ys[1:]:
            tot = tot + y
        mu = jnp.sum(tot, axis=-1, keepdims=True) * (1.0 / CONV_WIDTH)
        cs = [y - mu for y in ys]
        sq = cs[0] * cs[0]
        for c in cs[1:]:
            sq = sq + c * c
        var = jnp.sum(sq, axis=-1, keepdims=True) * (1.0 / CONV_WIDTH)
        inv = lax.rsqrt(var + GN_EPS)
        for l in range(CONV_SLABS):
            yn = cs[l] * inv * lgs[l] + lbs[l]
            obuf[l, pl.ds(j, CONV_SUB, stride=POUT), :] = _silu(yn)
        return carry

    lax.fori_loop(0, S, ln_body, 0, unroll=CONV_LN_UNROLL)

    for s in range(CONV_SUB):
        for l in range(CONV_SLABS):
            o_ref[s * S:(s + 1) * S, l * CONV_LANES:(l + 1) * CONV_LANES] = (
                obuf[l, s * POUT:s * POUT + S, :].astype(o_ref.dtype))


def _conv(u, conv_w, conv_b, ln_g, ln_b, batch, seq):
    tm = CONV_TM
    per_seq = seq // tm
    hb = tm // CONV_HALO
    nhb = seq // CONV_HALO

    def prev_map(b, i):
        return (b * nhb + jnp.maximum(i * hb - 1, 0), 0)

    def next_map(b, i):
        return (b * nhb + jnp.minimum((i + 1) * hb, nhb - 1), 0)

    return pl.pallas_call(
        _conv_kernel,
        grid=(batch, per_seq),
        in_specs=[
            pl.BlockSpec((CONV_HALO, CONV_WIDTH), prev_map),
            pl.BlockSpec((tm, CONV_WIDTH), lambda b, i: (b * per_seq + i, 0)),
            pl.BlockSpec((CONV_HALO, CONV_WIDTH), next_map),
            pl.BlockSpec((CONV_KERNEL, CONV_WIDTH), lambda b, i: (0, 0)),
            pl.BlockSpec((1, CONV_WIDTH), lambda b, i: (0, 0)),
            pl.BlockSpec((1, CONV_WIDTH), lambda b, i: (0, 0)),
            pl.BlockSpec((1, CONV_WIDTH), lambda b, i: (0, 0)),
        ],
        out_specs=pl.BlockSpec((tm, CONV_WIDTH), lambda b, i: (b * per_seq + i, 0)),
        out_shape=jax.ShapeDtypeStruct((batch * seq, CONV_WIDTH), BF16),
        scratch_shapes=[
            pltpu.VMEM((CONV_SLABS, CONV_SUB * CONV_PIN, CONV_LANES), F32),
            pltpu.VMEM((CONV_SLABS, CONV_TM, CONV_LANES), F32),
            pltpu.VMEM((CONV_SLABS, CONV_SUB * CONV_POUT, CONV_LANES), F32),
        ],
        compiler_params=_params(("arbitrary", "arbitrary")),
        name="conv",
    )(u, u, u, conv_w, conv_b, ln_g, ln_b)


RS_CS = 8


def _log_gamma(p):
    return jnp.log1p(-jnp.exp2(p))


def _rstate_kernel(kf_ref, vf_ref, kb_ref, vb_ref, pf_ref, pb_ref, rf_ref, rb_ref, sf_ref, sb_ref):
    n = pl.program_id(1)

    @pl.when(n == 0)
    def _init():
        sf_ref[...] = jnp.zeros_like(sf_ref)
        sb_ref[...] = jnp.zeros_like(sb_ref)

    row = lax.broadcasted_iota(jnp.int32, (CHUNK, RET_QK_DIM), 0).astype(F32)
    tdims = (((0,), (0,)), ((), ()))

    def scan(k_ref, v_ref, r_ref, s_ref, p_ref, order, fwd):
        for h in range(RET_HEADS):
            lg = _log_gamma(p_ref[h])
            pos = (CHUNK - 1.0 - row) if fwd else row
            kdec = jnp.exp(pos * lg)
            cdec = jnp.exp(CHUNK * lg)
            state = s_ref[h]
            for c in order:
                rows = slice(c * CHUNK, (c + 1) * CHUNK)
                r_ref[c, h] = state.astype(r_ref.dtype)
                kh = k_ref[rows, h * RET_QK_DIM:(h + 1) * RET_QK_DIM].astype(F32)
                kd = (kh * kdec).astype(BF16)
                vh = v_ref[rows, h * RET_V_DIM:(h + 1) * RET_V_DIM]
                kv = lax.dot_general(kd, vh, tdims, preferred_element_type=F32)
                state = state * cdec + kv
            s_ref[h] = state

    scan(kf_ref, vf_ref, rf_ref, sf_ref, pf_ref, tuple(range(RS_CS)), True)
    scan(kb_ref, vb_ref, rb_ref, sb_ref, pb_ref, tuple(reversed(range(RS_CS))), False)


def _rstate(z, pf, pb, batch, seq):
    tr = RS_CS * CHUNK
    ns = seq // tr
    nchunks = batch * seq // CHUNK
    state = jax.ShapeDtypeStruct((nchunks, RET_HEADS, RET_QK_DIM, RET_V_DIM), BF16)
    fmap = lambda b, n: (b * ns + n)
    bmap = lambda b, n: (b * ns + ns - 1 - n)
    return pl.pallas_call(
        _rstate_kernel,
        grid=(batch, ns),
        in_specs=[
            pl.BlockSpec((tr, RET_QK), lambda b, n: (fmap(b, n), ZB_K)),
            pl.BlockSpec((tr, RET_V), lambda b, n: (fmap(b, n), ZB2_V)),
            pl.BlockSpec((tr, RET_QK), lambda b, n: (bmap(b, n), ZB_K)),
            pl.BlockSpec((tr, RET_V), lambda b, n: (bmap(b, n), ZB2_V)),
            pl.BlockSpec((RET_HEADS, 1, 1), lambda b, n: (0, 0, 0)),
            pl.BlockSpec((RET_HEADS, 1, 1), lambda b, n: (0, 0, 0)),
        ],
        out_specs=[
            pl.BlockSpec((RS_CS, RET_HEADS, RET_QK_DIM, RET_V_DIM), lambda b, n: (fmap(b, n), 0, 0, 0)),
            pl.BlockSpec((RS_CS, RET_HEADS, RET_QK_DIM, RET_V_DIM), lambda b, n: (bmap(b, n), 0, 0, 0)),
        ],
        out_shape=[state, state],
        scratch_shapes=[
            pltpu.VMEM((RET_HEADS, RET_QK_DIM, RET_V_DIM), F32),
            pltpu.VMEM((RET_HEADS, RET_QK_DIM, RET_V_DIM), F32),
        ],
        compiler_params=_params(("arbitrary", "arbitrary")),
        name="rstate",
    )(z, z, z, z, pf, pb)


RO_CS = 8


def _rout_kernel(q_ref, k_ref, v_ref, gr_ref, rf_ref, rb_ref, pf_ref, pb_ref, gn_ref, o_ref):
    ri = lax.broadcasted_iota(jnp.int32, (CHUNK, CHUNK), 0)
    ci = lax.broadcasted_iota(jnp.int32, (CHUNK, CHUNK), 1)
    diff = (ri - ci).astype(F32)
    row = ri.astype(F32)
    ndims = (((1,), (1,)), ((), ()))
    for h in range(RET_HEADS):
        lgf = _log_gamma(pf_ref[h])
        lgb = _log_gamma(pb_ref[h])
        lower = diff >= 0
        upper = diff <= 0
        mask = (jnp.where(lower, jnp.exp(jnp.where(lower, diff, 0.0) * lgf), 0.0)
                + jnp.where(upper, jnp.exp(jnp.where(upper, -diff, 0.0) * lgb), 0.0))
        qdec_f = jnp.exp((row + 1.0) * lgf)
        qdec_b = jnp.exp((CHUNK - row) * lgb)
        gn = gn_ref[:, h * RET_V_DIM:(h + 1) * RET_V_DIM]
        for c in range(RO_CS):
            rows = slice(c * CHUNK, (c + 1) * CHUNK)
            q = q_ref[rows, h * RET_QK_DIM:(h + 1) * RET_QK_DIM]
            k = k_ref[rows, h * RET_QK_DIM:(h + 1) * RET_QK_DIM]
            v = v_ref[rows, h * RET_V_DIM:(h + 1) * RET_V_DIM]
            s = lax.dot_general(q, k, ndims, preferred_element_type=F32)
            inner = _dot((s * mask).astype(BF16), v)
            qf = q.astype(F32)
            qd = jnp.concatenate([(qf * qdec_f).astype(BF16), (qf * qdec_b).astype(BF16)], axis=1)
            rcat = jnp.concatenate([rf_ref[c, h], rb_ref[c, h]], axis=0)
            o = inner + _dot(qd, rcat)
            mu = jnp.mean(o, axis=-1, keepdims=True)
            cen = o - mu
            var = jnp.mean(cen * cen, axis=-1, keepdims=True)
            on = cen * lax.rsqrt(var + GN_EPS) * gn
            gr = gr_ref[rows, h * RET_V_DIM:(h + 1) * RET_V_DIM].astype(F32)
            o_ref[rows, h * RET_V_DIM:(h + 1) * RET_V_DIM] = (on * gr).astype(o_ref.dtype)


def _rout(z, rf, rb, pf, pb, gn_g):
    rows = z.shape[0]
    tr = RO_CS * CHUNK
    state_spec = pl.BlockSpec((RO_CS, RET_HEADS, RET_QK_DIM, RET_V_DIM), lambda i: (i, 0, 0, 0))
    return pl.pallas_call(
        _rout_kernel,
        grid=(rows // tr,),
        in_specs=[
            pl.BlockSpec((tr, RET_QK), lambda i: (i, ZB_Q)),
            pl.BlockSpec((tr, RET_QK), lambda i: (i, ZB_K)),
            pl.BlockSpec((tr, RET_V), lambda i: (i, ZB2_V)),
            pl.BlockSpec((tr, RET_V), lambda i: (i, ZB2_GR)),
            state_spec,
            state_spec,
            pl.BlockSpec((RET_HEADS, 1, 1), lambda i: (0, 0, 0)),
            pl.BlockSpec((RET_HEADS, 1, 1), lambda i: (0, 0, 0)),
            pl.BlockSpec((1, RET_V), lambda i: (0, 0)),
        ],
        out_specs=pl.BlockSpec((tr, RET_V), lambda i: (i, 0)),
        out_shape=jax.ShapeDtypeStruct((rows, RET_V), BF16),
        compiler_params=_params(("arbitrary",)),
        name="rout",
    )(z, z, z, z, rf, rb, pf, pb, gn_g)


MERGE_TM = 1024


def _merge_kernel(ca_ref, og_ref, sga_ref, sgb_ref, wc_ref, bc_ref, wr_ref, o_ref):
    for c in range(D_MODEL // MXU_N):
        cols = slice(c * MXU_N, (c + 1) * MXU_N)
        ya = _dot(ca_ref[...], wc_ref[:, cols]) + bc_ref[:, cols]
        yb = _dot(og_ref[...], wr_ref[:, cols])
        m = sga_ref[:, cols].astype(F32) * ya + sgb_ref[:, cols].astype(F32) * yb
        o_ref[:, cols] = m.astype(o_ref.dtype)


def _merge(cact, og, z, wc_b, bc, wr_b):
    rows = cact.shape[0]
    tm = MERGE_TM
    return pl.pallas_call(
        _merge_kernel,
        grid=(rows // tm,),
        in_specs=[
            pl.BlockSpec((tm, CONV_WIDTH), lambda i: (i, 0)),
            pl.BlockSpec((tm, RET_V), lambda i: (i, 0)),
            pl.BlockSpec((tm, D_MODEL), lambda i: (i, ZB2_SGA)),
            pl.BlockSpec((tm, D_MODEL), lambda i: (i, ZB2_SGB)),
            pl.BlockSpec((CONV_WIDTH, D_MODEL), lambda i: (0, 0), pipeline_mode=pl.Buffered(1)),
            pl.BlockSpec((1, D_MODEL), lambda i: (0, 0)),
            pl.BlockSpec((RET_V, D_MODEL), lambda i: (0, 0), pipeline_mode=pl.Buffered(1)),
        ],
        out_specs=pl.BlockSpec((tm, D_MODEL), lambda i: (i, 0)),
        out_shape=jax.ShapeDtypeStruct((rows, D_MODEL), BF16),
        compiler_params=_params(("arbitrary",)),
        name="merge",
    )(cact, og, z, z, wc_b, bc, wr_b)


OUTPROJ_TM = 512


def _outproj_kernel(m_ref, x_ref, wo_ref, gt_ref, g_ref, sc_ref, sh_ref, x1_ref, h2_ref):
    for c in range(D_MODEL // MXU_N):
        cols = slice(c * MXU_N, (c + 1) * MXU_N)
        mix = _dot(m_ref[...], wo_ref[:, cols])
        x1_ref[:, cols] = x_ref[:, cols] + gt_ref[:, cols] * mix
    h2_ref[...] = _modnorm_rows(x1_ref[...], g_ref[...], sc_ref[...], sh_ref[...]).astype(h2_ref.dtype)


def _outproj(merged, x2d, wo_b, mod3, norm2_g, seq):
    rows = merged.shape[0]
    tm = OUTPROJ_TM
    per_seq = seq // tm
    modspec = lambda blk: pl.BlockSpec((None, 1, D_MODEL), lambda i: (i // per_seq, 0, blk))
    return pl.pallas_call(
        _outproj_kernel,
        grid=(rows // tm,),
        in_specs=[
            pl.BlockSpec((tm, D_MODEL), lambda i: (i, 0)),
            pl.BlockSpec((tm, D_MODEL), lambda i: (i, 0)),
            pl.BlockSpec((D_MODEL, D_MODEL), lambda i: (0, 0)),
            modspec(2),
            pl.BlockSpec((1, D_MODEL), lambda i: (0, 0)),
            modspec(4),
            modspec(3),
        ],
        out_specs=[
            pl.BlockSpec((tm, D_MODEL), lambda i: (i, 0)),
            pl.BlockSpec((tm, D_MODEL), lambda i: (i, 0)),
        ],
        out_shape=[
            jax.ShapeDtypeStruct((rows, D_MODEL), F32),
            jax.ShapeDtypeStruct((rows, D_MODEL), BF16),
        ],
        compiler_params=_params(("arbitrary",)),
        name="outproj",
    )(merged, x2d, wo_b, mod3, norm2_g, mod3, mod3)


FFN_TM = 1024
FFN_TF = 512


def _ffn_kernel(h_ref, x1_hbm, wg_ref, wu_ref, wd_ref, gt_ref, fg_ref, o_ref, x1_buf, x1_sem):
    i = pl.program_id(0)
    f = pl.program_id(1)

    def x1_copy():
        return pltpu.make_async_copy(x1_hbm.at[pl.ds(i * FFN_TM, FFN_TM), :], x1_buf, x1_sem)

    @pl.when(f == 0)
    def _start_tile():
        x1_copy().start()
        o_ref[...] = jnp.zeros_like(o_ref)

    acts = []
    for c in range(FFN_TF // MXU_N):
        cols = slice(c * MXU_N, (c + 1) * MXU_N)
        g = _dot(h_ref[...], wg_ref[:, cols])
        u = _dot(h_ref[...], wu_ref[:, cols])
        acts.append((_silu(g) * u).astype(BF16))
    act = jnp.concatenate(acts, axis=1)
    for c in range(D_MODEL // MXU_N):
        cols = slice(c * MXU_N, (c + 1) * MXU_N)
        o_ref[:, cols] += _dot(act, wd_ref[:, cols])

    @pl.when(f == pl.num_programs(1) - 1)
    def _final():
        x1_copy().wait()
        x2 = x1_buf[...] + gt_ref[...] * o_ref[...]
        ms = jnp.mean(x2 * x2, axis=-1, keepdims=True)
        o_ref[...] = x2 * lax.rsqrt(ms + EPS) * fg_ref[...]


def _ffn(h2, x1, w_in_b, w_out_b, mod3, final_g, seq):
    rows = h2.shape[0]
    tm, tf = FFN_TM, FFN_TF
    per_seq = seq // tm
    return pl.pallas_call(
        _ffn_kernel,
        grid=(rows // tm, D_FF // tf),
        in_specs=[
            pl.BlockSpec((tm, D_MODEL), lambda i, f: (i, 0)),
            pl.BlockSpec(memory_space=pl.ANY),
            pl.BlockSpec((D_MODEL, tf), lambda i, f: (0, f)),
            pl.BlockSpec((D_MODEL, tf), lambda i, f: (0, f + D_FF // tf)),
            pl.BlockSpec((tf, D_MODEL), lambda i, f: (f, 0)),
            pl.BlockSpec((None, 1, D_MODEL), lambda i, f: (i // per_seq, 0, 5)),
            pl.BlockSpec((1, D_MODEL), lambda i, f: (0, 0)),
        ],
        out_specs=pl.BlockSpec((tm, D_MODEL), lambda i, f: (i, 0)),
        out_shape=jax.ShapeDtypeStruct((rows, D_MODEL), F32),
        scratch_shapes=[pltpu.VMEM((tm, D_MODEL), F32), pltpu.SemaphoreType.DMA(())],
        compiler_params=_params(("arbitrary", "arbitrary")),
        name="ffn",
    )(h2, x1, w_in_b, w_in_b, w_out_b, mod3, final_g)


def _rope_tables(seq):
    half = RET_QK_DIM // 2
    inv = ROPE_THETA ** (-jnp.arange(half, dtype=F32) / half)
    ang = jnp.arange(seq, dtype=F32)[:, None] * inv[None, :]
    cos = jnp.cos(ang)
    sin = jnp.sin(ang)
    return jnp.concatenate([cos, cos], axis=1), jnp.concatenate([-sin, sin], axis=1)


def _encode(x, mod, w):
    batch, seq, _ = x.shape
    x2d = x.reshape(batch * seq, D_MODEL)
    mod3 = mod.reshape(batch, 1, N_MOD * D_MODEL)
    cos2, sin2 = _rope_tables(seq)
    z, u = _inproj(x2d, w["norm1_g"], mod3, w["w_in"], cos2, sin2, seq)
    cact = _conv(u, w["conv_w"], w["conv_b"], w["ln_g"], w["ln_b"], batch, seq)
    rf, rb = _rstate(z, w["pf"], w["pb"], batch, seq)
    og = _rout(z, rf, rb, w["pf"], w["pb"], w["gn_g"])
    merged = _merge(cact, og, z, w["w_conv_out"], w["b_conv_out"], w["w_ret_out"])
    x1, h2 = _outproj(merged, x2d, w["w_out"], mod3, w["norm2_g"], seq)
    y = _ffn(h2, x1, w["w_ffn_in"], w["w_ffn_out"], mod3, w["final_g"], seq)
    return y.reshape(batch, seq, D_MODEL)


def kernel(x_prompt, x_sample, c_prompt, c_sample, norm1_g, norm2_g, w_ada, b_ada, w_in, conv_dw_w, conv_dw_b, conv_ln_g, conv_ln_b, w_conv_out, b_conv_out, ret_logdecay_fwd, ret_logdecay_bwd, ret_gn_g, w_ret_out, w_out, w_ffn_in, w_ffn_out, final_g):
    assert w_in.shape[0] == 1, "single-layer block"
    nb_p, nb_s = c_prompt.shape[0], c_sample.shape[0]
    pad = (-(nb_p + nb_s)) % 8
    c_all = jnp.concatenate([c_prompt, c_sample, jnp.zeros((pad, D_MODEL), F32)], axis=0)
    mod = _mod(c_all, w_ada[0], b_ada[0].reshape(1, -1))
    w = {
        "norm1_g": norm1_g[0].reshape(1, -1),
        "norm2_g": norm2_g[0].reshape(1, -1),
        "final_g": final_g.reshape(1, -1),
        "w_in": w_in[0].astype(BF16),
        "conv_w": conv_dw_w[0],
        "conv_b": conv_dw_b[0].reshape(1, -1),
        "ln_g": conv_ln_g[0].reshape(1, -1),
        "ln_b": conv_ln_b[0].reshape(1, -1),
        "w_conv_out": w_conv_out[0].astype(BF16),
        "b_conv_out": b_conv_out[0].reshape(1, -1),
        "pf": ret_logdecay_fwd[0].astype(F32).reshape(RET_HEADS, 1, 1),
        "pb": ret_logdecay_bwd[0].astype(F32).reshape(RET_HEADS, 1, 1),
        "gn_g": ret_gn_g[0].reshape(1, -1),
        "w_ret_out": w_ret_out[0].astype(BF16),
        "w_out": w_out[0].astype(BF16),
        "w_ffn_in": w_ffn_in[0].astype(BF16),
        "w_ffn_out": w_ffn_out[0].astype(BF16),
    }
    y_prompt = _encode(x_prompt, mod[:nb_p], w)
    y_sample = _encode(x_sample, mod[nb_p:nb_p + nb_s], w)
    return (y_prompt, y_sample)
```
